```python
import math
import jax, jax.numpy as jnp
from jax import lax
import numpy as np

D_MODEL = 4096
BATCH = 1
SEQ = 8192
DEPTH = 1

HEAD_DIM = 128
N_HEADS_FOX = 16
N_HEADS_DIL = 16
FOX_WIDTH = N_HEADS_FOX * HEAD_DIM
DIL_WIDTH = N_HEADS_DIL * HEAD_DIM
MIX_WIDTH = FOX_WIDTH + DIL_WIDTH
IN_PROJ_WIDTH = 3 * FOX_WIDTH + N_HEADS_FOX + 3 * DIL_WIDTH
DILATED_PATTERNS = ((128, 1), (512, 4), (2048, 16))
N_BUCKETS = 32
MAX_DISTANCE = 2048
D_FF = 11008
CONV_WIDTH = 3
Q_BLOCK = 128
RMS_EPS = 1e-6

kernel_name = "hymba_fox_dilated_convffn"


def _branch_offsets():
    offs = [np.arange(w // d + 1, dtype=np.int32) * d for (w, d) in DILATED_PATTERNS]
    lens = tuple(int(o.shape[0]) for o in offs)
    return np.concatenate(offs), lens


DIL_OFFSETS, DIL_BRANCH_LENS = _branch_offsets()


def rms_norm(x, g):
    xf = x.astype(jnp.float32)
    y = xf * lax.rsqrt(jnp.mean(xf * xf, axis=-1, keepdims=True) + RMS_EPS)
    return (y * g.astype(jnp.float32)).astype(x.dtype)


def t5_causal_bucket(dist):
    max_exact = N_BUCKETS // 2
    d_f = jnp.maximum(dist, 1).astype(jnp.float32)
    large = max_exact + (jnp.log(d_f / max_exact) / math.log(MAX_DISTANCE / max_exact)
                         * (N_BUCKETS - max_exact)).astype(jnp.int32)
    large = jnp.minimum(large, N_BUCKETS - 1)
    return jnp.where(dist < max_exact, dist, large)


def to_heads(t, n_heads):
    b, s, _ = t.shape
    return t.reshape(b, s, n_heads, HEAD_DIM).transpose(0, 2, 1, 3)


def to_blocks(t):
    b, h, s = t.shape[:3]
    t = t.reshape((b, h, s // Q_BLOCK, Q_BLOCK) + t.shape[3:])
    return jnp.moveaxis(t, 2, 0)


def from_blocks(t):
    nb, b, h, q, hd = t.shape
    t = jnp.moveaxis(t, 0, 2).reshape(b, h, nb * q, hd)
    return t.transpose(0, 2, 1, 3).reshape(b, nb * q, h * hd)


def forgetting_attention(q, k, v, log_f):
    s_len = q.shape[2]
    scale = HEAD_DIM ** -0.5
    c = jnp.cumsum(log_f, axis=-1)
    key_pos = jnp.arange(s_len, dtype=jnp.int32)
    starts = jnp.arange(s_len // Q_BLOCK, dtype=jnp.int32) * Q_BLOCK

    def block(args):
        qb, cb, t0 = args
        logits = jnp.einsum('bhqd,bhkd->bhqk', qb, k).astype(jnp.float32) * scale
        logits = logits + cb[..., None] - c[:, :, None, :]
        q_pos = t0 + jnp.arange(Q_BLOCK, dtype=jnp.int32)
        mask = key_pos[None, :] <= q_pos[:, None]
        logits = jnp.where(mask, logits, -jnp.inf)
        p = jax.nn.softmax(logits, axis=-1)
        return jnp.einsum('bhqk,bhkd->bhqd', p.astype(v.dtype), v)

    out = lax.map(block, (to_blocks(q), to_blocks(c), starts))
    return from_blocks(out)


def dilated_attention(q, k, v, rel_bias_table):
    s_len = q.shape[2]
    scale = HEAD_DIM ** -0.5
    offsets = jnp.asarray(DIL_OFFSETS)
    bias = rel_bias_table[t5_causal_bucket(offsets)].astype(jnp.float32).T
    starts = jnp.arange(s_len // Q_BLOCK, dtype=jnp.int32) * Q_BLOCK

    def block(args):
        qb, t0 = args
        q_pos = t0 + jnp.arange(Q_BLOCK, dtype=jnp.int32)
        k_pos = q_pos[:, None] - offsets[None, :]
        valid = k_pos >= 0
        k_idx = jnp.maximum(k_pos, 0)
        kg = k[:, :, k_idx]
        vg = v[:, :, k_idx]
        logits = jnp.einsum('bhqd,bhqjd->bhqj', qb, kg).astype(jnp.float32) * scale
        logits = logits + bias[None, :, None, :]
        logits = jnp.where(valid, logits, -jnp.inf)
        outs, lses = [], []
        start = 0
        for n in DIL_BRANCH_LENS:
            l = logits[..., start:start + n]
            m = jnp.max(l, axis=-1, keepdims=True)
            p = jnp.exp(l - m)
            den = jnp.sum(p, axis=-1, keepdims=True)
            o = jnp.einsum('bhqj,bhqjd->bhqd', (p / den).astype(vg.dtype), vg[..., start:start + n, :])
            outs.append(o.astype(jnp.float32))
            lses.append(m + jnp.log(den))
            start += n
        w = jax.nn.softmax(jnp.concatenate(lses, axis=-1), axis=-1)
        o = sum(w[..., i:i + 1] * outs[i] for i in range(len(outs)))
        return o.astype(qb.dtype)

    out = lax.map(block, (to_blocks(q), starts))
    return from_blocks(out)


def causal_dwconv(h, w, b):
    s_len = h.shape[1]
    hp = jnp.pad(h, ((0, 0), (CONV_WIDTH - 1, 0), (0, 0)))
    out = sum(hp[:, i:i + s_len, :] * w[i] for i in range(CONV_WIDTH))
    return out + b


def setup_inputs(seed: int = 0) -> dict:
    key = jax.random.key(seed)
    ks = jax.random.split(key, 16)
    f32 = jnp.float32
    nrm = lambda k, shape, s: jax.random.normal(k, shape, f32) * s
    return {
        "x": nrm(ks[0], (BATCH, SEQ, D_MODEL), 1.0),
        "attn_norm_g": 1.0 + nrm(ks[1], (DEPTH, D_MODEL), 0.02),
        "w_in": nrm(ks[2], (DEPTH, D_MODEL, IN_PROJ_WIDTH), D_MODEL ** -0.5),
        "fox_forget_b": 2.0 + nrm(ks[3], (DEPTH, N_HEADS_FOX), 0.1),
        "rel_bias_table": nrm(ks[4], (N_BUCKETS, N_HEADS_DIL), 0.1),
        "fox_out_norm_g": 1.0 + nrm(ks[5], (DEPTH, FOX_WIDTH), 0.02),
        "dil_out_norm_g": 1.0 + nrm(ks[6], (DEPTH, DIL_WIDTH), 0.02),
        "w_out": nrm(ks[7], (DEPTH, MIX_WIDTH, D_MODEL), MIX_WIDTH ** -0.5),
        "ffn_norm_g": 1.0 + nrm(ks[8], (DEPTH, D_MODEL), 0.02),
        "w_up": nrm(ks[9], (DEPTH, D_MODEL, 2 * D_FF), D_MODEL ** -0.5),
        "conv_w": nrm(ks[10], (DEPTH, CONV_WIDTH, D_FF), CONV_WIDTH ** -0.5),
        "conv_b": nrm(ks[11], (DEPTH, D_FF), 0.02),
        "w_down": nrm(ks[12], (DEPTH, D_FF, D_MODEL), D_FF ** -0.5),
        "final_norm_g": 1.0 + nrm(ks[13], (D_MODEL,), 0.02),
    }


def reference(x, attn_norm_g, w_in, fox_forget_b, rel_bias_table, fox_out_norm_g, dil_out_norm_g,
              w_out, ffn_norm_g, w_up, conv_w, conv_b, w_down, final_norm_g):
    c0 = 3 * FOX_WIDTH
    c1 = c0 + N_HEADS_FOX
    for l in range(DEPTH):
        h = rms_norm(x, attn_norm_g[l])
        proj = jnp.einsum('bsd,de->bse', h, w_in[l])
        q_a, k_a, v_a = (to_heads(t, N_HEADS_FOX) for t in jnp.split(proj[..., :c0], 3, axis=-1))
        f_logit = proj[..., c0:c1].astype(jnp.float32) + fox_forget_b[l].astype(jnp.float32)
        log_f = jax.nn.log_sigmoid(f_logit).transpose(0, 2, 1)
        q_b, k_b, v_b = (to_heads(t, N_HEADS_DIL) for t in jnp.split(proj[..., c1:], 3, axis=-1))

        out_a = forgetting_attention(q_a, k_a, v_a, log_f)
        out_b = dilated_attention(q_b, k_b, v_b, rel_bias_table)
        mixed = jnp.concatenate([rms_norm(out_a, fox_out_norm_g[l]),
                                 rms_norm(out_b, dil_out_norm_g[l])], axis=-1)
        x = x + jnp.einsum('bse,ed->bsd', mixed, w_out[l])

        hn = rms_norm(x, ffn_norm_g[l])
        up = jnp.einsum('bsd,df->bsf', hn, w_up[l])
        gate, val = up[..., :D_FF], up[..., D_FF:]
        gate = causal_dwconv(gate, conv_w[l], conv_b[l])
        x = x + jnp.einsum('bsf,fd->bsd', jax.nn.silu(gate) * val, w_down[l])
    return rms_norm(x, final_norm_g)
```

```python
import functools
import math

import jax
import jax.numpy as jnp
import numpy as np
from jax import lax
from jax.experimental import pallas as pl
from jax.experimental.pallas import tpu as pltpu

HEAD_DIM = 128
N_HEADS = 16
ATT_WIDTH = N_HEADS * HEAD_DIM
DIL_PATTERNS = ((128, 1), (512, 4), (2048, 16))
DIL_BLOCK = 128
DIL_SUPER = 2048
N_BUCKETS = 32
MAX_DISTANCE = 2048
RMS_EPS = 1e-6
CONV_WIDTH = 3
LANES = 128
SUBLANES = 8
VMEM_LIMIT = 56 * 1024 * 1024

F32 = jnp.float32
BF16 = jnp.bfloat16


def _dot(a, b):
    return jnp.dot(a, b, preferred_element_type=F32)


def _dot_nt(a, b):
    return lax.dot_general(a, b, (((1,), (1,)), ((), ())), preferred_element_type=F32)


def _rms_scale(x):
    return lax.rsqrt(jnp.mean(x * x, axis=-1, keepdims=True) + RMS_EPS)


def _cumsum_rows(v):
    n = v.shape[0]
    rows = lax.broadcasted_iota(jnp.int32, v.shape, 0)
    s = 1
    while s < n:
        v = v + jnp.where(rows >= s, pltpu.roll(v, s, axis=0), 0.0)
        s *= 2
    return v


def _scale_q_tiles(acc, j, n_q_tiles):
    return acc * jnp.where(j < n_q_tiles, HEAD_DIM ** -0.5, 1.0).astype(F32)


def _in_proj_fox_kernel(x_ref, g_ref, w_ref, wf_ref, fb_ref, o_ref, h_ref, c_ref, carry_scr, *, n_q_tiles):
    i = pl.program_id(0)
    j = pl.program_id(1)

    @pl.when(j == 0)
    def _():
        x = x_ref[...]
        hb = (x * _rms_scale(x) * g_ref[...]).astype(BF16)
        h_ref[...] = hb
        z = _dot(hb, wf_ref[...]) + fb_ref[...]
        log_f = jnp.minimum(z, 0.0) - jnp.log1p(jnp.exp(-jnp.abs(z)))

        @pl.when(i == 0)
        def _():
            carry_scr[...] = jnp.zeros(carry_scr.shape, F32)

        c = _cumsum_rows(log_f) + carry_scr[0:1, :]
        c_ref[...] = c
        carry_scr[...] = jnp.broadcast_to(c[c.shape[0] - 1:, :], carry_scr.shape)

    acc = _dot(h_ref[...], w_ref[...])
    o_ref[...] = _scale_q_tiles(acc, j, n_q_tiles).astype(o_ref.dtype)


def _in_proj_fox(x, g, w_qkv, w_f, f_b, tm, tn):
    s, d = x.shape
    kern = functools.partial(_in_proj_fox_kernel, n_q_tiles=ATT_WIDTH // tn)
    return pl.pallas_call(
        kern,
        grid=(s // tm, 3 * ATT_WIDTH // tn),
        in_specs=[
            pl.BlockSpec((tm, d), lambda i, j: (i, 0)),
            pl.BlockSpec((1, d), lambda i, j: (0, 0)),
            pl.BlockSpec((d, tn), lambda i, j: (0, j)),
            pl.BlockSpec((d, LANES), lambda i, j: (0, 0)),
            pl.BlockSpec((1, LANES), lambda i, j: (0, 0)),
        ],
        out_specs=[
            pl.BlockSpec((tm, tn), lambda i, j: (i, j)),
            pl.BlockSpec((tm, d), lambda i, j: (i, 0)),
            pl.BlockSpec((tm, LANES), lambda i, j: (i, 0)),
        ],
        out_shape=[
            jax.ShapeDtypeStruct((s, 3 * ATT_WIDTH), BF16),
            jax.ShapeDtypeStruct((s, d), BF16),
            jax.ShapeDtypeStruct((s, LANES), F32),
        ],
        scratch_shapes=[pltpu.VMEM((SUBLANES, LANES), F32)],
        compiler_params=pltpu.CompilerParams(
            dimension_semantics=("arbitrary", "arbitrary"), vmem_limit_bytes=VMEM_LIMIT),
        name="in_proj_fox",
    )(x, g, w_qkv, w_f, f_b)


def _in_proj_dil_kernel(h_ref, w_ref, o_ref, *, n_q_tiles):
    acc = _dot(h_ref[...], w_ref[...])
    o_ref[...] = _scale_q_tiles(acc, pl.program_id(1), n_q_tiles)


def _in_proj_dil(h, w_qkv, tm, tn):
    s, d = h.shape
    kern = functools.partial(_in_proj_dil_kernel, n_q_tiles=ATT_WIDTH // tn)
    return pl.pallas_call(
        kern,
        grid=(s // tm, 3 * ATT_WIDTH // tn),
        in_specs=[
            pl.BlockSpec((tm, d), lambda i, j: (i, 0)),
            pl.BlockSpec((d, tn), lambda i, j: (0, j)),
        ],
        out_specs=pl.BlockSpec((tm, tn), lambda i, j: (i, j)),
        out_shape=jax.ShapeDtypeStruct((s, 3 * ATT_WIDTH), F32),
        compiler_params=pltpu.CompilerParams(
            dimension_semantics=("arbitrary", "arbitrary"), vmem_limit_bytes=VMEM_LIMIT),
        name="in_proj_dil",
    )(h, w_qkv)


def _fox_kernel(q_ref, k_ref, v_ref, cq_ref, ck_ref, o_ref, acc_scr, m_scr, l_scr, *, tq, tk):
    i = pl.program_id(1)
    q = q_ref[...]
    cq = cq_ref[...]
    m_scr[...] = jnp.full(m_scr.shape, -jnp.inf, F32)
    l_scr[...] = jnp.zeros(l_scr.shape, F32)
    acc_scr[...] = jnp.zeros(acc_scr.shape, F32)

    def step(kb, masked):
        start = pl.multiple_of(kb * tk, tk)
        k = k_ref[pl.ds(start, tk), :]
        v = v_ref[pl.ds(start, tk), :]
        s = _dot_nt(q, k) + (cq - ck_ref[kb])
        if masked:
            rows = lax.broadcasted_iota(jnp.int32, s.shape, 0)
            cols = lax.broadcasted_iota(jnp.int32, s.shape, 1)
            s = jnp.where(cols <= rows, s, -jnp.inf)
        m_prev = m_scr[...]
        m_new = jnp.maximum(m_prev, jnp.max(s, axis=1, keepdims=True))
        alpha = jnp.exp(m_prev - m_new)
        p = jnp.exp(s - m_new)
        l_scr[...] = alpha * l_scr[...] + jnp.sum(p, axis=1, keepdims=True)
        acc_scr[...] = alpha * acc_scr[...] + _dot(p.astype(BF16), v)
        m_scr[...] = m_new

    def body(kb, carry):
        step(kb, False)
        return carry

    lax.fori_loop(0, i, body, 0)
    step(i, True)
    o_ref[...] = acc_scr[...] / l_scr[...]


def _fox_attention(qkv, c_q, c_k, tq):
    s = qkv.shape[0]
    tk = tq
    kern = functools.partial(_fox_kernel, tq=tq, tk=tk)
    return pl.pallas_call(
        kern,
        grid=(N_HEADS, s // tq),
        in_specs=[
            pl.BlockSpec((tq, HEAD_DIM), lambda h, i: (i, h)),
            pl.BlockSpec((s, HEAD_DIM), lambda h, i: (0, N_HEADS + h)),
            pl.BlockSpec((s, HEAD_DIM), lambda h, i: (0, 2 * N_HEADS + h)),
            pl.BlockSpec((None, tq, 1), lambda h, i: (h, i, 0)),
            pl.BlockSpec((None, s // tk, 1, tk), lambda h, i: (h, 0, 0, 0)),
        ],
        out_specs=pl.BlockSpec((tq, HEAD_DIM), lambda h, i: (i, h)),
        out_shape=jax.ShapeDtypeStruct((s, ATT_WIDTH), F32),
        scratch_shapes=[pltpu.VMEM((tq, HEAD_DIM), F32), pltpu.VMEM((tq, 1), F32),
                        pltpu.VMEM((tq, 1), F32)],
        compiler_params=pltpu.CompilerParams(
            dimension_semantics=("arbitrary", "arbitrary"), vmem_limit_bytes=VMEM_LIMIT),
        name="fox_attention",
    )(qkv, qkv, qkv, c_q, c_k)


def _t5_bucket_np(dist):
    max_exact = N_BUCKETS // 2
    d_f = np.maximum(dist, 1).astype(np.float32)
    ratio = np.log(d_f / np.float32(max_exact)) / np.float32(math.log(MAX_DISTANCE / max_exact))
    large = max_exact + (ratio * np.float32(N_BUCKETS - max_exact)).astype(np.int32)
    large = np.minimum(large, N_BUCKETS - 1)
    return np.where(dist < max_exact, dist, large).astype(np.int32)


def _dil_bias_tiles(rel_bias_table):
    qq = np.arange(DIL_BLOCK)[:, None]
    kk = np.arange(2 * DIL_BLOCK)[None, :]
    delta = qq + DIL_BLOCK - kk
    valid = (delta >= 0) & (delta <= DIL_BLOCK)
    tiles = []
    for _, dil in DIL_PATTERNS:
        bucket = _t5_bucket_np(np.where(valid, delta * dil, 0))
        t = rel_bias_table.astype(F32)[bucket]
        tiles.append(jnp.where(valid[..., None], t, -jnp.inf))
    return jnp.stack(tiles, axis=0).transpose(3, 0, 1, 2)


def _dil_block(q, k_prev, k_cur, v_prev, v_cur, bias):
    k2 = jnp.concatenate([k_prev, k_cur], axis=0)
    v2 = jnp.concatenate([v_prev, v_cur], axis=0)
    s = _dot_nt(q, k2) + bias
    m = jnp.max(s, axis=1, keepdims=True)
    p = jnp.exp(s - m)
    den = jnp.sum(p, axis=1, keepdims=True)
    o = _dot(p.astype(BF16), v2) * (1.0 / den)
    return o, m + jnp.log(den)


def _dil_kernel(q_ref, kp_ref, kc_ref, vp_ref, vc_ref, bias_ref, o_ref,
                o1, o2, o3, l1, l2, l3):
    first = pl.program_id(1) == 0
    o_scr = (o1, o2, o3)
    l_scr = (l1, l2, l3)
    neg_inf = jnp.full((DIL_BLOCK, DIL_BLOCK), -jnp.inf, F32)

    for br, (_, dil) in enumerate(DIL_PATTERNS):
        n_blocks = DIL_SUPER // (DIL_BLOCK * dil)
        bias = bias_ref[br]
        bias_first = jnp.concatenate(
            [jnp.where(first, neg_inf, bias[:, :DIL_BLOCK]), bias[:, DIL_BLOCK:]], axis=1)

        def rows(m, r, dil=dil):
            if dil == 1:
                return pl.ds(m * DIL_BLOCK, DIL_BLOCK)
            return pl.ds(m * DIL_BLOCK * dil + r, DIL_BLOCK, stride=dil)

        def residue(r, carry, br=br, n_blocks=n_blocks, bias=bias, bias_first=bias_first, rows=rows):
            k_prev = kp_ref[rows(n_blocks - 1, r), :].astype(BF16)
            v_prev = vp_ref[rows(n_blocks - 1, r), :].astype(BF16)
            for m in range(n_blocks):
                idx = rows(m, r)
                q = q_ref[idx, :].astype(BF16)
                k_cur = kc_ref[idx, :].astype(BF16)
                v_cur = vc_ref[idx, :].astype(BF16)
                o, lse = _dil_block(q, k_prev, k_cur, v_prev, v_cur, bias_first if m == 0 else bias)
                o_scr[br][idx, :] = o
                l_scr[br][idx, :] = jnp.broadcast_to(lse, (DIL_BLOCK, HEAD_DIM))
                k_prev, v_prev = k_cur, v_cur
            return carry

        if dil == 1:
            residue(0, 0)
        else:
            lax.fori_loop(0, dil, residue, 0)

    def merge(t, carry):
        idx = pl.ds(pl.multiple_of(t * DIL_BLOCK, DIL_BLOCK), DIL_BLOCK)
        la, lb, lc = l1[idx, :], l2[idx, :], l3[idx, :]
        mx = jnp.maximum(jnp.maximum(la, lb), lc)
        ea, eb, ec = jnp.exp(la - mx), jnp.exp(lb - mx), jnp.exp(lc - mx)
        num = ea * o1[idx, :] + eb * o2[idx, :] + ec * o3[idx, :]
        o_ref[idx, :] = num / (ea + eb + ec)
        return carry

    lax.fori_loop(0, DIL_SUPER // DIL_BLOCK, merge, 0)


def _dilated_attention(qkv, bias_tiles):
    s = qkv.shape[0]
    blk = (DIL_SUPER, HEAD_DIM)
    prev = lambda off: (lambda h, b: (jnp.maximum(b - 1, 0), off + h))
    cur = lambda off: (lambda h, b: (b, off + h))
    scratch = [pltpu.VMEM(blk, F32) for _ in range(6)]
    return pl.pallas_call(
        _dil_kernel,
        grid=(N_HEADS, s // DIL_SUPER),
        in_specs=[
            pl.BlockSpec(blk, cur(0)),
            pl.BlockSpec(blk, prev(N_HEADS)),
            pl.BlockSpec(blk, cur(N_HEADS)),
            pl.BlockSpec(blk, prev(2 * N_HEADS)),
            pl.BlockSpec(blk, cur(2 * N_HEADS)),
            pl.BlockSpec((None, len(DIL_PATTERNS), DIL_BLOCK, 2 * DIL_BLOCK), lambda h, b: (h, 0, 0, 0)),
        ],
        out_specs=pl.BlockSpec(blk, lambda h, b: (b, h)),
        out_shape=jax.ShapeDtypeStruct((s, ATT_WIDTH), F32),
        scratch_shapes=scratch,
        compiler_params=pltpu.CompilerParams(
            dimension_semantics=("arbitrary", "arbitrary"), vmem_limit_bytes=VMEM_LIMIT),
        name="dilated_attention",
    )(qkv, qkv, qkv, qkv, qkv, bias_tiles)


def _out_proj_kernel(a_ref, b_ref, ga_ref, gb_ref, w_ref, x_ref, o_ref, mix_scr):
    @pl.when(pl.program_id(1) == 0)
    def _():
        a = a_ref[...]
        b = b_ref[...]
        mix_scr[:, :ATT_WIDTH] = (a * _rms_scale(a) * ga_ref[...]).astype(BF16)
        mix_scr[:, ATT_WIDTH:] = (b * _rms_scale(b) * gb_ref[...]).astype(BF16)

    o_ref[...] = x_ref[...] + _dot(mix_scr[...], w_ref[...])


def _out_proj(a, b, ga, gb, w, x, tm, tn):
    s, d = x.shape
    return pl.pallas_call(
        _out_proj_kernel,
        grid=(s // tm, d // tn),
        in_specs=[
            pl.BlockSpec((tm, ATT_WIDTH), lambda i, j: (i, 0)),
            pl.BlockSpec((tm, ATT_WIDTH), lambda i, j: (i, 0)),
            pl.BlockSpec((1, ATT_WIDTH), lambda i, j: (0, 0)),
            pl.BlockSpec((1, ATT_WIDTH), lambda i, j: (0, 0)),
            pl.BlockSpec((2 * ATT_WIDTH, tn), lambda i, j: (0, j)),
            pl.BlockSpec((tm, tn), lambda i, j: (i, j)),
        ],
        out_specs=pl.BlockSpec((tm, tn), lambda i, j: (i, j)),
        out_shape=jax.ShapeDtypeStruct((s, d), F32),
        scratch_shapes=[pltpu.VMEM((tm, 2 * ATT_WIDTH), BF16)],
        compiler_params=pltpu.CompilerParams(
            dimension_semantics=("arbitrary", "arbitrary"), vmem_limit_bytes=VMEM_LIMIT),
        name="out_proj",
    )(a, b, ga, gb, w, x)


def _ffn_up_kernel(x_ref, g_ref, wg_ref, wv_ref, cw_ref, cb_ref, o_ref, h_scr, carry_scr):
    i = pl.program_id(0)
    j = pl.program_id(1)

    @pl.when(j == 0)
    def _():
        x = x_ref[...]
        h_scr[...] = (x * _rms_scale(x) * g_ref[...]).astype(BF16)

    h = h_scr[...]
    gate = _dot(h, wg_ref[...])
    val = _dot(h, wv_ref[...])
    tm = gate.shape[0]

    @pl.when(i == 0)
    def _():
        carry_scr[j] = jnp.zeros(carry_scr.shape[1:], F32)

    tail = carry_scr[j]
    carry_scr[j] = gate[tm - SUBLANES:, :]
    sub = lax.broadcasted_iota(jnp.int32, tail.shape, 0)

    def shifted(k):
        body = pltpu.roll(gate, k, axis=0)
        head = jnp.where(sub < k, pltpu.roll(tail, k, axis=0), body[:SUBLANES, :])
        return jnp.concatenate([head, body[SUBLANES:, :]], axis=0)

    cw = cw_ref[...]
    conv = shifted(2) * cw[0:1, :] + shifted(1) * cw[1:2, :] + gate * cw[2:3, :] + cb_ref[...]
    act = conv * (1.0 / (1.0 + jnp.exp(-conv))) * val
    o_ref[...] = act.astype(o_ref.dtype)


def _ffn_up(x, g, w_gate, w_val, conv_w, conv_b, tm, tn):
    s, d = x.shape
    fp = w_gate.shape[1]
    return pl.pallas_call(
        _ffn_up_kernel,
        grid=(s // tm, fp // tn),
        in_specs=[
            pl.BlockSpec((tm, d), lambda i, j: (i, 0)),
            pl.BlockSpec((1, d), lambda i, j: (0, 0)),
            pl.BlockSpec((d, tn), lambda i, j: (0, j)),
            pl.BlockSpec((d, tn), lambda i, j: (0, j)),
            pl.BlockSpec((CONV_WIDTH, tn), lambda i, j: (0, j)),
            pl.BlockSpec((1, tn), lambda i, j: (0, j)),
        ],
        out_specs=pl.BlockSpec((tm, tn), lambda i, j: (i, j)),
        out_shape=jax.ShapeDtypeStruct((s, fp), BF16),
        scratch_shapes=[pltpu.VMEM((tm, d), BF16), pltpu.VMEM((fp // tn, SUBLANES, tn), F32)],
        compiler_params=pltpu.CompilerParams(
            dimension_semantics=("arbitrary", "arbitrary"), vmem_limit_bytes=VMEM_LIMIT),
        name="ffn_up",
    )(x, g, w_gate, w_val, conv_w, conv_b)


def _ffn_down_kernel(a_ref, w_ref, x_ref, g_ref, o_ref, *, tn):
    k = pl.program_id(1)

    @pl.when(k == 0)
    def _():
        o_ref[...] = x_ref[...]

    a = a_ref[...]
    for n in range(o_ref.shape[1] // tn):
        cols = slice(n * tn, (n + 1) * tn)
        o_ref[:, cols] += _dot(a, w_ref[:, cols])

    @pl.when(k == pl.num_programs(1) - 1)
    def _():
        y = o_ref[...]
        o_ref[...] = y * _rms_scale(y) * g_ref[...]


def _ffn_down(act, w, x, g, tm, tk):
    s, d = x.shape
    fp = act.shape[1]
    return pl.pallas_call(
        functools.partial(_ffn_down_kernel, tn=min(512, d)),
        grid=(s // tm, fp // tk),
        in_specs=[
            pl.BlockSpec((tm, tk), lambda i, k: (i, k)),
            pl.BlockSpec((tk, d), lambda i, k: (k, 0)),
            pl.BlockSpec((tm, d), lambda i, k: (i, 0)),
            pl.BlockSpec((1, d), lambda i, k: (0, 0)),
        ],
        out_specs=pl.BlockSpec((tm, d), lambda i, k: (i, 0)),
        out_shape=jax.ShapeDtypeStruct((s, d), F32),
        compiler_params=pltpu.CompilerParams(
            dimension_semantics=("arbitrary", "arbitrary"), vmem_limit_bytes=VMEM_LIMIT),
        name="ffn_down",
    )(act, w, x, g)


def _tiles(s, f):
    tm = min(512, s)
    f_tile = 512 if f >= 512 else 256
    fp = -(-f // f_tile) * f_tile
    return tm, f_tile, fp


def _layer(x, attn_g, w_in, forget_b, bias_tiles, fox_g, dil_g, w_out, ffn_g, w_up, conv_w, conv_b,
           w_down, final_g):
    s, d = x.shape
    f = conv_w.shape[-1]
    tm, f_tile, fp = _tiles(s, f)
    c0 = 3 * ATT_WIDTH
    c1 = c0 + N_HEADS

    w_f = jnp.pad(w_in[:, c0:c1], ((0, 0), (0, LANES - N_HEADS))).astype(BF16)
    f_b = jnp.pad(forget_b.astype(F32), (0, LANES - N_HEADS))[None, :]

    qkv_a, h, c = _in_proj_fox(x, attn_g[None, :], w_in[:, :c0].astype(BF16), w_f, f_b, tm, 512)
    qkv_b = _in_proj_dil(h, w_in[:, c1:].astype(BF16), tm, 512)

    tq = min(512, s)
    c_heads = c[:, :N_HEADS].T
    c_q = c_heads[:, :, None]
    c_k = c_heads.reshape(N_HEADS, s // tq, 1, tq)
    out_a = _fox_attention(qkv_a, c_q, c_k, tq)
    out_b = _dilated_attention(qkv_b, bias_tiles)

    x1 = _out_proj(out_a, out_b, fox_g[None, :], dil_g[None, :], w_out.astype(BF16), x, tm, min(512, d))

    pad_f = ((0, 0), (0, fp - f))
    w_gate = jnp.pad(w_up[:, :f], pad_f).astype(BF16)
    w_val = jnp.pad(w_up[:, f:], pad_f).astype(BF16)
    act = _ffn_up(x1, ffn_g[None, :], w_gate, w_val, jnp.pad(conv_w, pad_f),
                  jnp.pad(conv_b, (0, fp - f))[None, :], tm, 256)
    w_dn = jnp.pad(w_down, ((0, fp - f), (0, 0))).astype(BF16)
    return _ffn_down(act, w_dn, x1, final_g[None, :], tm, f_tile)


def kernel(x, attn_norm_g, w_in, fox_forget_b, rel_bias_table, fox_out_norm_g, dil_out_norm_g, w_out,
           ffn_norm_g, w_up, conv_w, conv_b, w_down, final_norm_g):
    depth = w_in.shape[0]
    assert depth == 1 and x.shape[0] == 1, "single layer, single sequence"
    assert x.shape[1] % DIL_SUPER == 0
    bias_tiles = _dil_bias_tiles(rel_bias_table)
    y = _layer(x[0], attn_norm_g[0], w_in[0], fox_forget_b[0], bias_tiles, fox_out_norm_g[0],
               dil_out_norm_g[0], w_out[0], ffn_norm_g[0], w_up[0], conv_w[0], conv_b[0], w_down[0],
               final_norm_g)
    return y[None]
```

```python
import functools
import math

import jax
import jax.numpy as jnp
import numpy as np
from jax import lax
from jax.experimental import pallas as pl
from jax.experimental.pallas import tpu as pltpu

HEAD_DIM = 128
N_HEADS = 16
ATT_WIDTH = N_HEADS * HEAD_DIM
DIL_PATTERNS = ((128, 1), (512, 4), (2048, 16))
DIL_BLOCK = 128
DIL_SUPER = 2048
N_BUCKETS = 32
MAX_DISTANCE = 2048
RMS_EPS = 1e-6
CONV_WIDTH = 3
LOG2E = math.log2(math.e)
N_BIAS_COLS = 3
LANES = 128
SUBLANES = 8
VMEM_LIMIT = 56 * 1024 * 1024

F32 = jnp.float32
BF16 = jnp.bfloat16


def _dot(a, b):
    return jnp.dot(a, b, preferred_element_type=F32)


def _dot_nt(a, b):
    return lax.dot_general(a, b, (((1,), (1,)), ((), ())), preferred_element_type=F32)


def _rms_scale(x):
    return lax.rsqrt(jnp.mean(x * x, axis=-1, keepdims=True) + RMS_EPS)


def _cumsum_rows(v):
    n = v.shape[0]
    rows = lax.broadcasted_iota(jnp.int32, v.shape, 0)
    s = 1
    while s < n:
        v = v + jnp.where(rows >= s, pltpu.roll(v, s, axis=0), 0.0)
        s *= 2
    return v


def _params(n_grid_dims, vmem_limit=VMEM_LIMIT):
    return pltpu.CompilerParams(dimension_semantics=("arbitrary",) * n_grid_dims,
                                vmem_limit_bytes=vmem_limit)


def _norm_gate_kernel(x_ref, g_ref, wf_ref, fb_ref, h_ref, c_ref, carry_scr):
    x = x_ref[...]
    hb = (x * _rms_scale(x) * g_ref[...]).astype(BF16)
    h_ref[...] = hb
    z = _dot(hb, wf_ref[...]) + fb_ref[...]
    log_f = jnp.minimum(z, 0.0) - jnp.log1p(jnp.exp(-jnp.abs(z)))

    @pl.when(pl.program_id(0) == 0)
    def _():
        carry_scr[...] = jnp.zeros(carry_scr.shape, F32)

    c = _cumsum_rows(log_f) + carry_scr[0:1, :]
    c_ref[...] = c
    carry_scr[...] = jnp.broadcast_to(c[c.shape[0] - 1:, :], carry_scr.shape)


def _norm_gate(x, g, w_f, f_b, tm):
    s, d = x.shape
    return pl.pallas_call(
        _norm_gate_kernel,
        grid=(s // tm,),
        in_specs=[
            pl.BlockSpec((tm, d), lambda i: (i, 0)),
            pl.BlockSpec((1, d), lambda i: (0, 0)),
            pl.BlockSpec((d, LANES), lambda i: (0, 0)),
            pl.BlockSpec((1, LANES), lambda i: (0, 0)),
        ],
        out_specs=[pl.BlockSpec((tm, d), lambda i: (i, 0)), pl.BlockSpec((tm, LANES), lambda i: (i, 0))],
        out_shape=[jax.ShapeDtypeStruct((s, d), BF16), jax.ShapeDtypeStruct((s, LANES), F32)],
        scratch_shapes=[pltpu.VMEM((SUBLANES, LANES), F32)],
        compiler_params=_params(1),
        name="norm_gate",
    )(x, g, w_f, f_b)


def _in_proj_fox_kernel(h_ref, w_ref, o_ref, w_scr, *, n_q_tiles):
    j = pl.program_id(0)

    @pl.when(pl.program_id(1) == 0)
    def _():
        w_scr[...] = w_ref[...].astype(BF16)

    scale = jnp.where(j < n_q_tiles, HEAD_DIM ** -0.5 * LOG2E, 1.0).astype(F32)
    o_ref[...] = (_dot(h_ref[...], w_scr[...]) * scale).astype(o_ref.dtype)


def _in_proj_fox(h, w_in, tm, tn):
    s, d = h.shape
    kern = functools.partial(_in_proj_fox_kernel, n_q_tiles=ATT_WIDTH // tn)
    return pl.pallas_call(
        kern,
        grid=(3 * ATT_WIDTH // tn, s // tm),
        in_specs=[
            pl.BlockSpec((tm, d), lambda j, i: (i, 0)),
            pl.BlockSpec((d, tn), lambda j, i: (0, j)),
        ],
        out_specs=pl.BlockSpec((tm, tn), lambda j, i: (i, j)),
        out_shape=jax.ShapeDtypeStruct((s, 3 * ATT_WIDTH), BF16),
        scratch_shapes=[pltpu.VMEM((d, tn), BF16)],
        compiler_params=_params(2),
        name="in_proj_fox",
    )(h, w_in)


def _in_proj_dil_kernel(h_ref, w_ref, o_ref, *, n_q_tiles):
    scale = jnp.where(pl.program_id(1) < n_q_tiles, HEAD_DIM ** -0.5 * LOG2E, 1.0).astype(F32)
    o_ref[...] = _dot(h_ref[...], w_ref[...]) * scale


def _in_proj_dil(h, w_qkv, tm, tn):
    s, d = h.shape
    kern = functools.partial(_in_proj_dil_kernel, n_q_tiles=ATT_WIDTH // tn)
    return pl.pallas_call(
        kern,
        grid=(s // tm, 3 * ATT_WIDTH // tn),
        in_specs=[
            pl.BlockSpec((tm, d), lambda i, j: (i, 0)),
            pl.BlockSpec((d, tn), lambda i, j: (0, j)),
        ],
        out_specs=pl.BlockSpec((tm, tn), lambda i, j: (i, j)),
        out_shape=jax.ShapeDtypeStruct((s, 3 * ATT_WIDTH), F32),
        compiler_params=_params(2),
        name="in_proj_dil",
    )(h, w_qkv)


def _split_bf16x3(v):
    hi = lax.reduce_precision(v, exponent_bits=8, mantissa_bits=7)
    rest = v - hi
    mid = lax.reduce_precision(rest, exponent_bits=8, mantissa_bits=7)
    return hi, mid, rest - mid


def _fox_key_bias(c):
    ck = c[:, :N_HEADS].T * (-LOG2E)
    cols = jnp.stack(_split_bf16x3(ck), axis=-1).astype(BF16)
    return jnp.pad(cols, ((0, 0), (0, 0), (0, HEAD_DIM - N_BIAS_COLS)))


def _fox_kernel(q_ref, k_ref, kc_ref, v_ref, o_ref, acc_scr, m_scr, *, tq, tk, ts):
    i = pl.program_id(1)
    lane = lax.broadcasted_iota(jnp.int32, (ts, HEAD_DIM), 1)
    bias_ones = (lane < N_BIAS_COLS).astype(BF16)
    ones = jnp.ones((tk, HEAD_DIM), BF16)
    m_scr[...] = jnp.full(m_scr.shape, -jnp.inf, F32)
    acc_scr[...] = jnp.zeros(acc_scr.shape, F32)

    def step(kb, diag):
        rows_k = pl.ds(pl.multiple_of(kb * tk, tk), tk)
        k_aug = jnp.concatenate([k_ref[rows_k, :], kc_ref[rows_k, :]], axis=1)
        v_aug = jnp.concatenate([v_ref[rows_k, :], ones], axis=1)
        for r0 in range(0, tq, ts):
            if diag is not None and r0 + ts - 1 < diag * tk:
                continue
            rs = slice(r0, r0 + ts)
            nk = tk if diag is None else min(tk, r0 + ts - diag * tk)
            q_aug = jnp.concatenate([q_ref[rs, :], bias_ones], axis=1)
            s = _dot_nt(q_aug, k_aug[:nk])
            if diag is not None and r0 < diag * tk + nk - 1:
                rows = lax.broadcasted_iota(jnp.int32, s.shape, 0)
                cols = lax.broadcasted_iota(jnp.int32, s.shape, 1)
                s = jnp.where(cols + (diag * tk - r0) <= rows, s, -jnp.inf)
            chunks = [s[:, c * LANES:(c + 1) * LANES] for c in range(nk // LANES)]
            m_cur = functools.reduce(jnp.maximum, chunks)
            m_prev = m_scr[rs, :]
            m_new = jnp.maximum(m_prev, jnp.max(m_cur, axis=1, keepdims=True))
            alpha = jnp.exp2(m_prev - m_new)
            p = jnp.concatenate([jnp.exp2(ch - m_new) for ch in chunks], axis=1).astype(BF16)
            acc_scr[rs, :] = acc_scr[rs, :] * jnp.concatenate([alpha, alpha], axis=1) + _dot(p, v_aug[:nk])
            m_scr[rs, :] = m_new

    def pair(t, carry):
        step(2 * t, None)
        step(2 * t + 1, None)
        return carry

    n_diag = tq // tk
    n_full = i * n_diag
    lax.fori_loop(0, n_full // 2, pair, 0)

    @pl.when(n_full % 2 == 1)
    def _():
        step(n_full - 1, None)

    for dg in range(n_diag):
        step(i * n_diag + dg, dg)
    acc = acc_scr[...]
    o_ref[...] = acc[:, :HEAD_DIM] / acc[:, HEAD_DIM:]


def _fox_attention(qkv, kc, tq, tk, ts):
    s = qkv.shape[0]
    kern = functools.partial(_fox_kernel, tq=tq, tk=tk, ts=ts)
    return pl.pallas_call(
        kern,
        grid=(N_HEADS, s // tq),
        in_specs=[
            pl.BlockSpec((tq, HEAD_DIM), lambda h, i: (i, h)),
            pl.BlockSpec((s, HEAD_DIM), lambda h, i: (0, N_HEADS + h)),
            pl.BlockSpec((None, s, HEAD_DIM), lambda h, i: (h, 0, 0)),
            pl.BlockSpec((s, HEAD_DIM), lambda h, i: (0, 2 * N_HEADS + h)),
        ],
        out_specs=pl.BlockSpec((tq, HEAD_DIM), lambda h, i: (i, h)),
        out_shape=jax.ShapeDtypeStruct((s, ATT_WIDTH), F32),
        scratch_shapes=[pltpu.VMEM((tq, 2 * HEAD_DIM), F32), pltpu.VMEM((tq, HEAD_DIM), F32)],
        compiler_params=_params(2),
        name="fox_attention",
    )(qkv, qkv, kc, qkv)


def _t5_bucket_np(dist):
    max_exact = N_BUCKETS // 2
    d_f = np.maximum(dist, 1).astype(np.float32)
    ratio = np.log(d_f / np.float32(max_exact)) / np.float32(math.log(MAX_DISTANCE / max_exact))
    large = max_exact + (ratio * np.float32(N_BUCKETS - max_exact)).astype(np.int32)
    large = np.minimum(large, N_BUCKETS - 1)
    return np.where(dist < max_exact, dist, large).astype(np.int32)


def _dil_bias_tiles(rel_bias_table):
    n_valid = DIL_BLOCK + 1
    dist = np.stack([(DIL_BLOCK - np.arange(n_valid)) * dil for _, dil in DIL_PATTERNS])
    row = rel_bias_table.astype(F32)[_t5_bucket_np(dist)]
    row = jnp.transpose(row, (2, 0, 1)) * LOG2E
    row = jnp.concatenate([row, jnp.full(row.shape[:2] + (DIL_BLOCK,), -jnp.inf, F32)], axis=-1)
    flat = jnp.tile(row, (1, 1, DIL_BLOCK))[..., :DIL_BLOCK * 2 * DIL_BLOCK]
    return flat.reshape(row.shape[:2] + (DIL_BLOCK, 2 * DIL_BLOCK))


def _dil_block(q, k_prev, k_cur, v_prev, v_cur, bias):
    k2 = jnp.concatenate([k_prev, k_cur], axis=0)
    v2 = jnp.concatenate([v_prev, v_cur], axis=0)
    s = _dot_nt(q, k2) + bias
    m = jnp.max(s, axis=1, keepdims=True)
    p = jnp.exp2(s - m)
    den = jnp.sum(p, axis=1, keepdims=True)
    o = _dot(p.astype(BF16), v2) * (1.0 / den)
    return o, m + jnp.log(den) * LOG2E


def _dil_kernel(q_ref, kp_ref, kc_ref, vp_ref, vc_ref, bias_ref, o_ref,
                o1, o2, o3, l1, l2, l3):
    first = pl.program_id(1) == 0
    o_scr = (o1, o2, o3)
    l_scr = (l1, l2, l3)
    neg_inf = jnp.full((DIL_BLOCK, DIL_BLOCK), -jnp.inf, F32)

    for br, (_, dil) in enumerate(DIL_PATTERNS):
        n_blocks = DIL_SUPER // (DIL_BLOCK * dil)
        bias = bias_ref[br]
        bias_first = jnp.concatenate(
            [jnp.where(first, neg_inf, bias[:, :DIL_BLOCK]), bias[:, DIL_BLOCK:]], axis=1)

        def rows(m, r, dil=dil):
            if dil == 1:
                return pl.ds(m * DIL_BLOCK, DIL_BLOCK)
            return pl.ds(m * DIL_BLOCK * dil + r, DIL_BLOCK, stride=dil)

        def residue(r, carry, br=br, n_blocks=n_blocks, bias=bias, bias_first=bias_first, rows=rows):
            k_prev = kp_ref[rows(n_blocks - 1, r), :].astype(BF16)
            v_prev = vp_ref[rows(n_blocks - 1, r), :].astype(BF16)
            for m in range(n_blocks):
                idx = rows(m, r)
                q = q_ref[idx, :].astype(BF16)
                k_cur = kc_ref[idx, :].astype(BF16)
                v_cur = vc_ref[idx, :].astype(BF16)
                o, lse = _dil_block(q, k_prev, k_cur, v_prev, v_cur, bias_first if m == 0 else bias)
                o_scr[br][idx, :] = o
                l_scr[br][idx, :] = jnp.broadcast_to(lse, (DIL_BLOCK, HEAD_DIM))
                k_prev, v_prev = k_cur, v_cur
            return carry

        for r in range(dil):
            residue(r, 0)

    def merge(t, carry):
        idx = pl.ds(pl.multiple_of(t * DIL_BLOCK, DIL_BLOCK), DIL_BLOCK)
        la, lb, lc = l1[idx, :], l2[idx, :], l3[idx, :]
        mx = jnp.maximum(jnp.maximum(la, lb), lc)
        ea, eb, ec = jnp.exp2(la - mx), jnp.exp2(lb - mx), jnp.exp2(lc - mx)
        num = ea * o1[idx, :] + eb * o2[idx, :] + ec * o3[idx, :]
        o_ref[idx, :] = num / (ea + eb + ec)
        return carry

    lax.fori_loop(0, DIL_SUPER // DIL_BLOCK, merge, 0)


def _dilated_attention(qkv, bias_tiles):
    s = qkv.shape[0]
    blk = (DIL_SUPER, HEAD_DIM)
    prev = lambda off: (lambda h, b: (jnp.maximum(b - 1, 0), off + h))
    cur = lambda off: (lambda h, b: (b, off + h))
    scratch = [pltpu.VMEM(blk, F32) for _ in range(6)]
    return pl.pallas_call(
        _dil_kernel,
        grid=(N_HEADS, s // DIL_SUPER),
        in_specs=[
            pl.BlockSpec(blk, cur(0)),
            pl.BlockSpec(blk, prev(N_HEADS)),
            pl.BlockSpec(blk, cur(N_HEADS)),
            pl.BlockSpec(blk, prev(2 * N_HEADS)),
            pl.BlockSpec(blk, cur(2 * N_HEADS)),
            pl.BlockSpec((None, len(DIL_PATTERNS), DIL_BLOCK, 2 * DIL_BLOCK), lambda h, b: (h, 0, 0, 0)),
        ],
        out_specs=pl.BlockSpec(blk, lambda h, b: (b, h)),
        out_shape=jax.ShapeDtypeStruct((s, ATT_WIDTH), F32),
        scratch_shapes=scratch,
        compiler_params=_params(2),
        name="dilated_attention",
    )(qkv, qkv, qkv, qkv, qkv, bias_tiles)


def _out_proj_kernel(a_ref, b_ref, ga_ref, gb_ref, w_ref, x_ref, gf_ref, o_ref, xg_ref, rs_ref, mix_scr,
                     *, d_model):
    j = pl.program_id(1)

    @pl.when(j == 0)
    def _():
        a = a_ref[...]
        b = b_ref[...]
        mix_scr[:, :ATT_WIDTH] = (a * _rms_scale(a) * ga_ref[...]).astype(BF16)
        mix_scr[:, ATT_WIDTH:] = (b * _rms_scale(b) * gb_ref[...]).astype(BF16)
        rs_ref[...] = jnp.zeros(rs_ref.shape, F32)

    x1 = x_ref[...] + _dot(mix_scr[...], w_ref[...])
    o_ref[...] = x1
    xg_ref[...] = (x1 * gf_ref[...]).astype(BF16)
    rs_ref[...] += jnp.sum(x1 * x1, axis=1, keepdims=True)

    @pl.when(j == pl.num_programs(1) - 1)
    def _():
        rs_ref[...] = lax.rsqrt(rs_ref[...] * (1.0 / d_model) + RMS_EPS)


def _out_proj(a, b, ga, gb, w, x, g_ffn, tm, tn):
    s, d = x.shape
    row = lambda i, j: (i, 0)
    tile = lambda i, j: (i, j)
    return pl.pallas_call(
        functools.partial(_out_proj_kernel, d_model=d),
        grid=(s // tm, d // tn),
        in_specs=[
            pl.BlockSpec((tm, ATT_WIDTH), row),
            pl.BlockSpec((tm, ATT_WIDTH), row),
            pl.BlockSpec((1, ATT_WIDTH), lambda i, j: (0, 0)),
            pl.BlockSpec((1, ATT_WIDTH), lambda i, j: (0, 0)),
            pl.BlockSpec((2 * ATT_WIDTH, tn), lambda i, j: (0, j)),
            pl.BlockSpec((tm, tn), tile),
            pl.BlockSpec((1, tn), lambda i, j: (0, j)),
        ],
        out_specs=[pl.BlockSpec((tm, tn), tile), pl.BlockSpec((tm, tn), tile), pl.BlockSpec((tm, LANES), row)],
        out_shape=[jax.ShapeDtypeStruct((s, d), F32), jax.ShapeDtypeStruct((s, d), BF16),
                   jax.ShapeDtypeStruct((s, LANES), F32)],
        scratch_shapes=[pltpu.VMEM((tm, 2 * ATT_WIDTH), BF16)],
        compiler_params=_params(2),
        name="out_proj",
    )(a, b, ga, gb, w, x, g_ffn)


def _ffn_up_kernel(xg_ref, rs_ref, wg_ref, wv_ref, cw_ref, cb_ref, o_ref, wg_scr, wv_scr, tail_scr,
                   *, rows_per_chunk):
    @pl.when(pl.program_id(1) == 0)
    def _():
        wg_scr[...] = wg_ref[...].astype(BF16)
        wv_scr[...] = wv_ref[...].astype(BF16)
        tail_scr[...] = jnp.zeros(tail_scr.shape, F32)

    tm, tn = o_ref.shape
    cw = cw_ref[...]
    cb = cb_ref[...]
    tail = tail_scr[...]
    sub = lax.broadcasted_iota(jnp.int32, tail.shape, 0)
    for r0 in range(0, tm, rows_per_chunk):
        rows = slice(r0, r0 + rows_per_chunk)
        xg = xg_ref[rows, :]
        rs = jnp.concatenate([rs_ref[rows, :]] * (tn // LANES), axis=1)
        gate = _dot(xg, wg_scr[...]) * rs
        val = _dot(xg, wv_scr[...]) * rs

        def shifted(k, gate=gate, tail=tail):
            body = pltpu.roll(gate, k, axis=0)
            head = jnp.where(sub < k, pltpu.roll(tail, k, axis=0), body[:SUBLANES, :])
            return jnp.concatenate([head, body[SUBLANES:, :]], axis=0)

        conv = shifted(2) * cw[0:1, :] + shifted(1) * cw[1:2, :] + gate * cw[2:3, :] + cb
        act = conv * (1.0 / (1.0 + jnp.exp(-conv))) * val
        o_ref[rows, :] = act.astype(o_ref.dtype)
        tail = gate[rows_per_chunk - SUBLANES:, :]
    tail_scr[...] = tail


def _ffn_up(xg, rs, w_up, conv_w, conv_b, tm, tn):
    s, d = xg.shape
    f = conv_w.shape[1]
    n_f = f // tn
    return pl.pallas_call(
        functools.partial(_ffn_up_kernel, rows_per_chunk=min(256, tm)),
        grid=(n_f, s // tm),
        in_specs=[
            pl.BlockSpec((tm, d), lambda j, i: (i, 0)),
            pl.BlockSpec((tm, LANES), lambda j, i: (i, 0)),
            pl.BlockSpec((d, tn), lambda j, i: (0, j)),
            pl.BlockSpec((d, tn), lambda j, i: (0, n_f + j)),
            pl.BlockSpec((CONV_WIDTH, tn), lambda j, i: (0, j)),
            pl.BlockSpec((1, tn), lambda j, i: (0, j)),
        ],
        out_specs=pl.BlockSpec((tm, tn), lambda j, i: (i, j)),
        out_shape=jax.ShapeDtypeStruct((s, f), BF16),
        scratch_shapes=[pltpu.VMEM((d, tn), BF16), pltpu.VMEM((d, tn), BF16), pltpu.VMEM((SUBLANES, tn), F32)],
        compiler_params=_params(2),
        name="ffn_up",
    )(xg, rs, w_up, w_up, conv_w, conv_b)


def _ffn_down_kernel(a_ref, w_ref, x_ref, g_ref, o_ref, *, tn):
    n = pl.program_id(1)
    cols = pl.ds(pl.multiple_of(n * tn, tn), tn)
    o_ref[:, cols] = x_ref[...] + _dot(a_ref[...], w_ref[...])

    @pl.when(n == pl.num_programs(1) - 1)
    def _():
        y = o_ref[...]
        o_ref[...] = y * _rms_scale(y) * g_ref[...]


def _ffn_down(act, w, x, g, tm, tn):
    s, d = x.shape
    f = act.shape[1]
    return pl.pallas_call(
        functools.partial(_ffn_down_kernel, tn=tn),
        grid=(s // tm, d // tn),
        in_specs=[
            pl.BlockSpec((tm, f), lambda i, n: (i, 0)),
            pl.BlockSpec((f, tn), lambda i, n: (0, n)),
            pl.BlockSpec((tm, tn), lambda i, n: (i, n)),
            pl.BlockSpec((1, d), lambda i, n: (0, 0)),
        ],
        out_specs=pl.BlockSpec((tm, d), lambda i, n: (i, 0)),
        out_shape=jax.ShapeDtypeStruct((s, d), F32),
        compiler_params=_params(2, vmem_limit=60 * 1024 * 1024),
        name="ffn_down",
    )(act, w, x, g)


def _layer(x, attn_g, w_in, forget_b, bias_tiles, fox_g, dil_g, w_out, ffn_g, w_up, conv_w, conv_b,
           w_down, final_g):
    s, d = x.shape
    c0 = 3 * ATT_WIDTH
    c1 = c0 + N_HEADS
    tm_small = min(512, s)
    tm_big = min(1024, s)

    w_f = jnp.pad(w_in[:, c0:c1], ((0, 0), (0, LANES - N_HEADS))).astype(BF16)
    f_b = jnp.pad(forget_b.astype(F32), (0, LANES - N_HEADS))[None, :]
    h, c = _norm_gate(x, attn_g[None, :], w_f, f_b, tm_small)

    qkv_a = _in_proj_fox(h, w_in, tm_big, 512)
    qkv_b = _in_proj_dil(h, w_in[:, c1:].astype(BF16), tm_big, 1024)

    out_a = _fox_attention(qkv_a, _fox_key_bias(c), tm_big, tm_big, 256)
    out_b = _dilated_attention(qkv_b, bias_tiles)

    x1, xg, rs = _out_proj(out_a, out_b, fox_g[None, :], dil_g[None, :], w_out.astype(BF16), x,
                           ffn_g[None, :], tm_small, min(1024, d))
    act = _ffn_up(xg, rs, w_up, conv_w, conv_b[None, :], tm_big, 256)
    return _ffn_down(act, w_down.astype(BF16), x1, final_g[None, :], tm_small, min(256, d))


def kernel(x, attn_norm_g, w_in, fox_forget_b, rel_bias_table, fox_out_norm_g, dil_out_norm_g, w_out,
           ffn_norm_g, w_up, conv_w, conv_b, w_down, final_norm_g):
    depth = w_in.shape[0]
    assert depth == 1 and x.shape[0] == 1, "single layer, single sequence"
    assert x.shape[1] % DIL_SUPER == 0 and conv_w.shape[-1] % 256 == 0
    bias_tiles = _dil_bias_tiles(rel_bias_table)
    y = _layer(x[0], attn_norm_g[0], w_in[0], fox_forget_b[0], bias_tiles, fox_out_norm_g[0],
               dil_out_norm_g[0], w_out[0], ffn_norm_g[0], w_up[0], conv_w[0], conv_b[0], w_down[0],
               final_norm_g)
    return y[None]
```

```python
import functools
import math

import jax
import jax.numpy as jnp
import numpy as np
from jax import lax
from jax.experimental import pallas as pl
from jax.experimental.pallas import tpu as pltpu

HEAD_DIM = 128
N_HEADS = 16
ATT_WIDTH = N_HEADS * HEAD_DIM
DIL_PATTERNS = ((128, 1), (512, 4), (2048, 16))
DIL_BLOCK = 128
DIL_SUPER = 2048
DIL_RESIDUE_GROUP = 16
N_BUCKETS = 32
MAX_DISTANCE = 2048
RMS_EPS = 1e-6
CONV_WIDTH = 3
LOG2E = math.log2(math.e)
N_BIAS_COLS = 3
LANES = 128
SUBLANES = 8
VMEM_LIMIT = 56 * 1024 * 1024

F32 = jnp.float32
BF16 = jnp.bfloat16


def _dot(a, b):
    return jnp.dot(a, b, preferred_element_type=F32)


def _dot_nt(a, b):
    return lax.dot_general(a, b, (((1,), (1,)), ((), ())), preferred_element_type=F32)


def _rms_scale(x):
    return lax.rsqrt(jnp.mean(x * x, axis=-1, keepdims=True) + RMS_EPS)


def _cumsum_rows(v):
    n = v.shape[0]
    rows = lax.broadcasted_iota(jnp.int32, v.shape, 0)
    s = 1
    while s < n:
        v = v + jnp.where(rows >= s, pltpu.roll(v, s, axis=0), 0.0)
        s *= 2
    return v


def _params(n_grid_dims, vmem_limit=VMEM_LIMIT):
    return pltpu.CompilerParams(dimension_semantics=("arbitrary",) * n_grid_dims,
                                vmem_limit_bytes=vmem_limit)


def _norm_gate_kernel(x_ref, g_ref, wf_ref, fb_ref, h_ref, c_ref, carry_scr):
    x = x_ref[...]
    hb = (x * _rms_scale(x) * g_ref[...]).astype(BF16)
    h_ref[...] = hb
    z = _dot_nt(hb, wf_ref[...]) + fb_ref[...]
    log_f = jnp.minimum(z, 0.0) - jnp.log1p(jnp.exp(-jnp.abs(z)))

    @pl.when(pl.program_id(0) == 0)
    def _():
        carry_scr[...] = jnp.zeros(carry_scr.shape, F32)

    c = _cumsum_rows(log_f) + carry_scr[0:1, :]
    c_ref[...] = c
    carry_scr[...] = jnp.broadcast_to(c[c.shape[0] - 1:, :], carry_scr.shape)


def _norm_gate(x, g, w_f, f_b, tm):
    s, d = x.shape
    return pl.pallas_call(
        _norm_gate_kernel,
        grid=(s // tm,),
        in_specs=[
            pl.BlockSpec((tm, d), lambda i: (i, 0)),
            pl.BlockSpec((1, d), lambda i: (0, 0)),
            pl.BlockSpec((LANES, d), lambda i: (0, 0)),
            pl.BlockSpec((1, LANES), lambda i: (0, 0)),
        ],
        out_specs=[pl.BlockSpec((tm, d), lambda i: (i, 0)), pl.BlockSpec((tm, LANES), lambda i: (i, 0))],
        out_shape=[jax.ShapeDtypeStruct((s, d), BF16), jax.ShapeDtypeStruct((s, LANES), F32)],
        scratch_shapes=[pltpu.VMEM((SUBLANES, LANES), F32)],
        compiler_params=_params(1),
        name="norm_gate",
    )(x, g, w_f, f_b)


def _in_proj_kernel(h_ref, wt_ref, o_ref, w_scr, *, n_q_tiles):
    j = pl.program_id(0)

    @pl.when(pl.program_id(1) == 0)
    def _():
        w_scr[...] = wt_ref[...].astype(BF16)

    scale = jnp.where(j < n_q_tiles, HEAD_DIM ** -0.5 * LOG2E, 1.0).astype(F32)
    o_ref[...] = (_dot_nt(h_ref[...], w_scr[...]) * scale).astype(o_ref.dtype)


def _in_proj(h, w_in_t, row0, out_dtype, tm, tn, name):
    s, d = h.shape
    kern = functools.partial(_in_proj_kernel, n_q_tiles=ATT_WIDTH // tn)
    return pl.pallas_call(
        kern,
        grid=(3 * ATT_WIDTH // tn, s // tm),
        in_specs=[
            pl.BlockSpec((tm, d), lambda j, i: (i, 0)),
            pl.BlockSpec((pl.Element(tn), pl.Element(d)),
                         lambda j, i: (pl.multiple_of(row0 + j * tn, SUBLANES), 0)),
        ],
        out_specs=pl.BlockSpec((tm, tn), lambda j, i: (i, j)),
        out_shape=jax.ShapeDtypeStruct((s, 3 * ATT_WIDTH), out_dtype),
        scratch_shapes=[pltpu.VMEM((tn, d), BF16)],
        compiler_params=_params(2),
        name=name,
    )(h, w_in_t)


def _split_bf16x3(v):
    hi = lax.reduce_precision(v, exponent_bits=8, mantissa_bits=7)
    rest = v - hi
    mid = lax.reduce_precision(rest, exponent_bits=8, mantissa_bits=7)
    return hi, mid, rest - mid


def _fox_key_bias(c):
    ck = (c[:, :N_HEADS].T * (-LOG2E))[:, :, None]
    hi, mid, lo = _split_bf16x3(ck)
    lane = lax.broadcasted_iota(jnp.int32, (1, 1, HEAD_DIM), 2)
    cols = jnp.where(lane == 0, hi, jnp.where(lane == 1, mid, jnp.where(lane == 2, lo, 0.0)))
    return cols.astype(BF16)


def _fox_kernel(q_ref, k_ref, kc_ref, v_ref, o_ref, acc_scr, m_scr, *, tq, tk, ts):
    i = pl.program_id(1)
    lane = lax.broadcasted_iota(jnp.int32, (ts, HEAD_DIM), 1)
    bias_ones = (lane < N_BIAS_COLS).astype(BF16)
    ones = jnp.ones((tk, HEAD_DIM), BF16)
    m_scr[...] = jnp.full(m_scr.shape, -jnp.inf, F32)
    acc_scr[...] = jnp.zeros(acc_scr.shape, F32)

    def step(kb, diag):
        rows_k = pl.ds(pl.multiple_of(kb * tk, tk), tk)
        k_aug = jnp.concatenate([k_ref[rows_k, :], kc_ref[rows_k, :]], axis=1)
        v_aug = jnp.concatenate([v_ref[rows_k, :], ones], axis=1)
        for r0 in range(0, tq, ts):
            if diag is not None and r0 + ts - 1 < diag * tk:
                continue
            rs = slice(r0, r0 + ts)
            nk = tk if diag is None else min(tk, r0 + ts - diag * tk)
            q_aug = jnp.concatenate([q_ref[rs, :], bias_ones], axis=1)
            s = _dot_nt(q_aug, k_aug[:nk])
            if diag is not None and r0 < diag * tk + nk - 1:
                rows = lax.broadcasted_iota(jnp.int32, s.shape, 0)
                cols = lax.broadcasted_iota(jnp.int32, s.shape, 1)
                s = jnp.where(cols + (diag * tk - r0) <= rows, s, -jnp.inf)
            chunks = [s[:, c * LANES:(c + 1) * LANES] for c in range(nk // LANES)]
            m_cur = functools.reduce(jnp.maximum, chunks)
            m_prev = m_scr[rs, :]
            m_new = jnp.maximum(m_prev, jnp.max(m_cur, axis=1, keepdims=True))
            alpha = jnp.exp2(m_prev - m_new)
            p = jnp.concatenate([jnp.exp2(ch - m_new) for ch in chunks], axis=1).astype(BF16)
            acc_scr[rs, :] = acc_scr[rs, :] * jnp.concatenate([alpha, alpha], axis=1) + _dot(p, v_aug[:nk])
            m_scr[rs, :] = m_new

    def pair(t, carry):
        step(2 * t, None)
        step(2 * t + 1, None)
        return carry

    n_diag = tq // tk
    n_full = i * n_diag
    lax.fori_loop(0, n_full // 2, pair, 0)

    @pl.when(n_full % 2 == 1)
    def _():
        step(n_full - 1, None)

    for dg in range(n_diag):
        step(i * n_diag + dg, dg)
    acc = acc_scr[...]
    o_ref[...] = (acc[:, :HEAD_DIM] / acc[:, HEAD_DIM:]).astype(o_ref.dtype)


def _fox_attention(qkv, kc, tq, tk, ts):
    s = qkv.shape[0]
    kern = functools.partial(_fox_kernel, tq=tq, tk=tk, ts=ts)
    return pl.pallas_call(
        kern,
        grid=(N_HEADS, s // tq),
        in_specs=[
            pl.BlockSpec((tq, HEAD_DIM), lambda h, i: (i, h)),
            pl.BlockSpec((s, HEAD_DIM), lambda h, i: (0, N_HEADS + h)),
            pl.BlockSpec((None, s, HEAD_DIM), lambda h, i: (h, 0, 0)),
            pl.BlockSpec((s, HEAD_DIM), lambda h, i: (0, 2 * N_HEADS + h)),
        ],
        out_specs=pl.BlockSpec((tq, HEAD_DIM), lambda h, i: (i, h)),
        out_shape=jax.ShapeDtypeStruct((s, ATT_WIDTH), BF16),
        scratch_shapes=[pltpu.VMEM((tq, 2 * HEAD_DIM), F32), pltpu.VMEM((tq, HEAD_DIM), F32)],
        compiler_params=_params(2),
        name="fox_attention",
    )(qkv, qkv, kc, qkv)


def _t5_bucket_np(dist):
    max_exact = N_BUCKETS // 2
    d_f = np.maximum(dist, 1).astype(np.float32)
    ratio = np.log(d_f / np.float32(max_exact)) / np.float32(math.log(MAX_DISTANCE / max_exact))
    large = max_exact + (ratio * np.float32(N_BUCKETS - max_exact)).astype(np.int32)
    large = np.minimum(large, N_BUCKETS - 1)
    return np.where(dist < max_exact, dist, large).astype(np.int32)


def _dil_bias_tiles(rel_bias_table):
    n_valid = DIL_BLOCK + 1
    dist = np.stack([(DIL_BLOCK - np.arange(n_valid)) * dil for _, dil in DIL_PATTERNS])
    row = rel_bias_table.astype(F32)[_t5_bucket_np(dist)]
    row = jnp.transpose(row, (2, 0, 1)) * LOG2E
    row = jnp.concatenate([row, jnp.full(row.shape[:2] + (DIL_BLOCK,), -jnp.inf, F32)], axis=-1)
    flat = jnp.tile(row, (1, 1, DIL_BLOCK))[..., :DIL_BLOCK * 2 * DIL_BLOCK]
    return flat.reshape(row.shape[:2] + (DIL_BLOCK, 2 * DIL_BLOCK))


def _dil_block(q, k_prev, k_cur, v_prev, v_cur, bias):
    k2 = jnp.concatenate([k_prev, k_cur], axis=0)
    v2 = jnp.concatenate([v_prev, v_cur], axis=0)
    s = _dot_nt(q, k2) + bias
    m = jnp.max(s, axis=1, keepdims=True)
    p = jnp.exp2(s - m)
    den = jnp.sum(p, axis=1, keepdims=True)
    o = _dot(p.astype(BF16), v2) * (1.0 / den)
    return o, m + jnp.log(den) * LOG2E


def _dil_kernel(q_ref, kp_ref, kc_ref, vp_ref, vc_ref, bias_ref, o_ref,
                o1, o2, o3, l1, l2, l3):
    first = pl.program_id(1) == 0
    o_scr = (o1, o2, o3)
    l_scr = (l1, l2, l3)
    neg_inf = jnp.full((DIL_BLOCK, DIL_BLOCK), -jnp.inf, F32)

    for br, (_, dil) in enumerate(DIL_PATTERNS):
        n_blocks = DIL_SUPER // (DIL_BLOCK * dil)
        bias = bias_ref[br]
        bias_first = jnp.concatenate(
            [jnp.where(first, neg_inf, bias[:, :DIL_BLOCK]), bias[:, DIL_BLOCK:]], axis=1)

        def rows(m, r, dil=dil):
            if dil == 1:
                return pl.ds(m * DIL_BLOCK, DIL_BLOCK)
            return pl.ds(m * DIL_BLOCK * dil + r, DIL_BLOCK, stride=dil)

        def residue(r, carry, br=br, n_blocks=n_blocks, bias=bias, bias_first=bias_first, rows=rows):
            k_prev = kp_ref[rows(n_blocks - 1, r), :].astype(BF16)
            v_prev = vp_ref[rows(n_blocks - 1, r), :].astype(BF16)
            for m in range(n_blocks):
                idx = rows(m, r)
                q = q_ref[idx, :].astype(BF16)
                k_cur = kc_ref[idx, :].astype(BF16)
                v_cur = vc_ref[idx, :].astype(BF16)
                o, lse = _dil_block(q, k_prev, k_cur, v_prev, v_cur, bias_first if m == 0 else bias)
                o_scr[br][idx, :] = o
                l_scr[br][idx, :] = jnp.broadcast_to(lse, (DIL_BLOCK, HEAD_DIM))
                k_prev, v_prev = k_cur, v_cur
            return carry

        group = min(dil, DIL_RESIDUE_GROUP)

        def residue_group(g, carry, residue=residue, group=group):
            for r in range(group):
                residue(g * group + r, carry)
            return carry

        if dil == group:
            residue_group(0, 0)
        else:
            lax.fori_loop(0, dil // group, residue_group, 0)

    def merge(t, carry):
        idx = pl.ds(pl.multiple_of(t * DIL_BLOCK, DIL_BLOCK), DIL_BLOCK)
        la, lb, lc = l1[idx, :], l2[idx, :], l3[idx, :]
        mx = jnp.maximum(jnp.maximum(la, lb), lc)
        ea, eb, ec = jnp.exp2(la - mx), jnp.exp2(lb - mx), jnp.exp2(lc - mx)
        num = ea * o1[idx, :] + eb * o2[idx, :] + ec * o3[idx, :]
        o_ref[idx, :] = (num / (ea + eb + ec)).astype(o_ref.dtype)
        return carry

    lax.fori_loop(0, DIL_SUPER // DIL_BLOCK, merge, 0)


def _dilated_attention(qkv, bias_tiles):
    s = qkv.shape[0]
    blk = (DIL_SUPER, HEAD_DIM)
    prev = lambda off: (lambda h, b: (jnp.maximum(b - 1, 0), off + h))
    cur = lambda off: (lambda h, b: (b, off + h))
    scratch = [pltpu.VMEM(blk, F32) for _ in range(6)]
    return pl.pallas_call(
        _dil_kernel,
        grid=(N_HEADS, s // DIL_SUPER),
        in_specs=[
            pl.BlockSpec(blk, cur(0)),
            pl.BlockSpec(blk, prev(N_HEADS)),
            pl.BlockSpec(blk, cur(N_HEADS)),
            pl.BlockSpec(blk, prev(2 * N_HEADS)),
            pl.BlockSpec(blk, cur(2 * N_HEADS)),
            pl.BlockSpec((None, len(DIL_PATTERNS), DIL_BLOCK, 2 * DIL_BLOCK), lambda h, b: (h, 0, 0, 0)),
        ],
        out_specs=pl.BlockSpec(blk, lambda h, b: (b, h)),
        out_shape=jax.ShapeDtypeStruct((s, ATT_WIDTH), BF16),
        scratch_shapes=scratch,
        compiler_params=_params(2),
        name="dilated_attention",
    )(qkv, qkv, qkv, qkv, qkv, bias_tiles)


def _out_proj_kernel(a_ref, b_ref, ga_ref, gb_ref, w_ref, x_ref, gf_ref, o_ref, xg_ref, rs_ref, mix_scr,
                     *, d_model):
    j = pl.program_id(1)

    @pl.when(j == 0)
    def _():
        a = a_ref[...].astype(F32)
        b = b_ref[...].astype(F32)
        mix_scr[:, :ATT_WIDTH] = (a * _rms_scale(a) * ga_ref[...]).astype(BF16)
        mix_scr[:, ATT_WIDTH:] = (b * _rms_scale(b) * gb_ref[...]).astype(BF16)
        rs_ref[...] = jnp.zeros(rs_ref.shape, F32)

    x1 = x_ref[...] + _dot(mix_scr[...], w_ref[...])
    o_ref[...] = x1
    xg_ref[...] = (x1 * gf_ref[...]).astype(BF16)
    rs_ref[...] += jnp.sum(x1 * x1, axis=1, keepdims=True)

    @pl.when(j == pl.num_programs(1) - 1)
    def _():
        rs_ref[...] = lax.rsqrt(rs_ref[...] * (1.0 / d_model) + RMS_EPS)


def _out_proj(a, b, ga, gb, w, x, g_ffn, tm, tn):
    s, d = x.shape
    row = lambda i, j: (i, 0)
    tile = lambda i, j: (i, j)
    return pl.pallas_call(
        functools.partial(_out_proj_kernel, d_model=d),
        grid=(s // tm, d // tn),
        in_specs=[
            pl.BlockSpec((tm, ATT_WIDTH), row),
            pl.BlockSpec((tm, ATT_WIDTH), row),
            pl.BlockSpec((1, ATT_WIDTH), lambda i, j: (0, 0)),
            pl.BlockSpec((1, ATT_WIDTH), lambda i, j: (0, 0)),
            pl.BlockSpec((2 * ATT_WIDTH, tn), lambda i, j: (0, j)),
            pl.BlockSpec((tm, tn), tile),
            pl.BlockSpec((1, tn), lambda i, j: (0, j)),
        ],
        out_specs=[pl.BlockSpec((tm, tn), tile), pl.BlockSpec((tm, tn), tile), pl.BlockSpec((tm, LANES), row)],
        out_shape=[jax.ShapeDtypeStruct((s, d), F32), jax.ShapeDtypeStruct((s, d), BF16),
                   jax.ShapeDtypeStruct((s, LANES), F32)],
        scratch_shapes=[pltpu.VMEM((tm, 2 * ATT_WIDTH), BF16)],
        compiler_params=_params(2),
        name="out_proj",
    )(a, b, ga, gb, w, x, g_ffn)


def _ffn_up_kernel(xg_ref, rs_ref, wg_ref, wv_ref, cw_ref, cb_ref, o_ref, wg_scr, wv_scr, tail_scr,
                   *, rows_per_chunk):
    @pl.when(pl.program_id(1) == 0)
    def _():
        wg_scr[...] = wg_ref[...].astype(BF16)
        wv_scr[...] = wv_ref[...].astype(BF16)
        tail_scr[...] = jnp.zeros(tail_scr.shape, F32)

    tm, tn = o_ref.shape
    cw = cw_ref[...]
    cb = cb_ref[...]
    tail = tail_scr[...]
    sub = lax.broadcasted_iota(jnp.int32, tail.shape, 0)
    for r0 in range(0, tm, rows_per_chunk):
        rows = slice(r0, r0 + rows_per_chunk)
        xg = xg_ref[rows, :]
        rs = jnp.concatenate([rs_ref[rows, :]] * (tn // LANES), axis=1)
        gate = _dot(xg, wg_scr[...]) * rs
        val = _dot(xg, wv_scr[...]) * rs

        def shifted(k, gate=gate, tail=tail):
            body = pltpu.roll(gate, k, axis=0)
            head = jnp.where(sub < k, pltpu.roll(tail, k, axis=0), body[:SUBLANES, :])
            return jnp.concatenate([head, body[SUBLANES:, :]], axis=0)

        conv = shifted(2) * cw[0:1, :] + shifted(1) * cw[1:2, :] + gate * cw[2:3, :] + cb
        act = conv * (1.0 / (1.0 + jnp.exp(-conv))) * val
        o_ref[rows, :] = act.astype(o_ref.dtype)
        tail = gate[rows_per_chunk - SUBLANES:, :]
    tail_scr[...] = tail


def _ffn_up(xg, rs, w_up, conv_w, conv_b, tm, tn):
    s, d = xg.shape
    f = conv_w.shape[1]
    n_f = f // tn
    return pl.pallas_call(
        functools.partial(_ffn_up_kernel, rows_per_chunk=min(256, tm)),
        grid=(n_f, s // tm),
        in_specs=[
            pl.BlockSpec((tm, d), lambda j, i: (i, 0)),
            pl.BlockSpec((tm, LANES), lambda j, i: (i, 0)),
            pl.BlockSpec((d, tn), lambda j, i: (0, j)),
            pl.BlockSpec((d, tn), lambda j, i: (0, n_f + j)),
            pl.BlockSpec((CONV_WIDTH, tn), lambda j, i: (0, j)),
            pl.BlockSpec((1, tn), lambda j, i: (0, j)),
        ],
        out_specs=pl.BlockSpec((tm, tn), lambda j, i: (i, j)),
        out_shape=jax.ShapeDtypeStruct((s, f), BF16),
        scratch_shapes=[pltpu.VMEM((d, tn), BF16), pltpu.VMEM((d, tn), BF16), pltpu.VMEM((SUBLANES, tn), F32)],
        compiler_params=_params(2),
        name="ffn_up",
    )(xg, rs, w_up, w_up, conv_w, conv_b)


def _ffn_down_kernel(a_ref, w_ref, x_ref, g_ref, o_ref, *, tn):
    n = pl.program_id(1)
    cols = pl.ds(pl.multiple_of(n * tn, tn), tn)
    o_ref[:, cols] = x_ref[...] + _dot(a_ref[...], w_ref[...])

    @pl.when(n == pl.num_programs(1) - 1)
    def _():
        y = o_ref[...]
        o_ref[...] = y * _rms_scale(y) * g_ref[...]


def _ffn_down(act, w, x, g, tm, tn):
    s, d = x.shape
    f = act.shape[1]
    return pl.pallas_call(
        functools.partial(_ffn_down_kernel, tn=tn),
        grid=(s // tm, d // tn),
        in_specs=[
            pl.BlockSpec((tm, f), lambda i, n: (i, 0)),
            pl.BlockSpec((f, tn), lambda i, n: (0, n)),
            pl.BlockSpec((tm, tn), lambda i, n: (i, n)),
            pl.BlockSpec((1, d), lambda i, n: (0, 0)),
        ],
        out_specs=pl.BlockSpec((tm, d), lambda i, n: (i, 0)),
        out_shape=jax.ShapeDtypeStruct((s, d), F32),
        compiler_params=_params(2, vmem_limit=60 * 1024 * 1024),
        name="ffn_down",
    )(act, w, x, g)


def _layer(x, attn_g, w_in, forget_b, bias_tiles, fox_g, dil_g, w_out, ffn_g, w_up, conv_w, conv_b,
           w_down, final_g):
    s, d = x.shape
    c0 = 3 * ATT_WIDTH
    c1 = c0 + N_HEADS
    tm_small = min(512, s)
    tm_big = min(1024, s)

    w_in_t = w_in.T
    w_f = jnp.pad(w_in_t[c0:c1], ((0, LANES - N_HEADS), (0, 0))).astype(BF16)
    f_b = jnp.pad(forget_b.astype(F32), (0, LANES - N_HEADS))[None, :]
    h, c = _norm_gate(x, attn_g[None, :], w_f, f_b, tm_small)

    qkv_a = _in_proj(h, w_in_t, 0, BF16, tm_big, 512, "in_proj_fox")
    qkv_b = _in_proj(h, w_in_t, c1, F32, tm_big, 512, "in_proj_dil")

    out_a = _fox_attention(qkv_a, _fox_key_bias(c), min(2048, s), tm_big, 256)
    out_b = _dilated_attention(qkv_b, bias_tiles)

    x1, xg, rs = _out_proj(out_a, out_b, fox_g[None, :], dil_g[None, :], w_out.astype(BF16), x,
                           ffn_g[None, :], tm_big, min(512, d))
    act = _ffn_up(xg, rs, w_up, conv_w, conv_b[None, :], tm_big, 256)
    return _ffn_down(act, w_down.astype(BF16), x1, final_g[None, :], tm_small, min(256, d))


def kernel(x, attn_norm_g, w_in, fox_forget_b, rel_bias_table, fox_out_norm_g, dil_out_norm_g, w_out,
           ffn_norm_g, w_up, conv_w, conv_b, w_down, final_norm_g):
    depth = w_in.shape[0]
    assert depth == 1 and x.shape[0] == 1, "single layer, single sequence"
    assert x.shape[1] % DIL_SUPER == 0 and conv_w.shape[-1] % 256 == 0
    bias_tiles = _dil_bias_tiles(rel_bias_table)
    y = _layer(x[0], attn_norm_g[0], w_in[0], fox_forget_b[0], bias_tiles, fox_out_norm_g[0],
               dil_out_norm_g[0], w_out[0], ffn_norm_g[0], w_up[0], conv_w[0], conv_b[0], w_down[0],
               final_norm_g)
    return y[None]
```

```python
import functools
import math

import jax
import jax.numpy as jnp
import numpy as np
from jax import lax
from jax.experimental import pallas as pl
from jax.experimental.pallas import tpu as pltpu

HEAD_DIM = 128
N_HEADS = 16
ATT_WIDTH = N_HEADS * HEAD_DIM
DIL_PATTERNS = ((128, 1), (512, 4), (2048, 16))
DIL_BLOCK = 128
DIL_SUPER = 2048
DIL_RESIDUE_GROUP = 16
N_BUCKETS = 32
MAX_DISTANCE = 2048
RMS_EPS = 1e-6
CONV_WIDTH = 3
LOG2E = math.log2(math.e)
N_BIAS_COLS = 3
LANES = 128
SUBLANES = 8
VMEM_LIMIT = 56 * 1024 * 1024

F32 = jnp.float32
BF16 = jnp.bfloat16
U32 = jnp.uint32


def _dot(a, b):
    return jnp.dot(a, b, preferred_element_type=F32)


def _dot_nt(a, b):
    return lax.dot_general(a, b, (((1,), (1,)), ((), ())), preferred_element_type=F32)


def _pack_rows(v):
    return pltpu.bitcast(v, U32)


def _unpack_rows(v):
    return pltpu.bitcast(v, BF16)


def _rms_scale(x):
    return lax.rsqrt(jnp.mean(x * x, axis=-1, keepdims=True) + RMS_EPS)


def _cumsum_rows(v):
    n = v.shape[0]
    rows = lax.broadcasted_iota(jnp.int32, v.shape, 0)
    s = 1
    while s < n:
        v = v + jnp.where(rows >= s, pltpu.roll(v, s, axis=0), 0.0)
        s *= 2
    return v


def _params(n_grid_dims, vmem_limit=VMEM_LIMIT):
    return pltpu.CompilerParams(dimension_semantics=("arbitrary",) * n_grid_dims,
                                vmem_limit_bytes=vmem_limit)


def _norm_gate_kernel(x_ref, g_ref, wf_ref, fb_ref, h_ref, c_ref, carry_scr):
    x = x_ref[...]
    hb = (x * _rms_scale(x) * g_ref[...]).astype(BF16)
    h_ref[...] = _pack_rows(hb)
    z = _dot_nt(hb, wf_ref[...]) + fb_ref[...]
    log_f = jnp.minimum(z, 0.0) - jnp.log1p(jnp.exp(-jnp.abs(z)))

    @pl.when(pl.program_id(0) == 0)
    def _():
        carry_scr[...] = jnp.zeros(carry_scr.shape, F32)

    c = _cumsum_rows(log_f) + carry_scr[0:1, :]
    c_ref[...] = c
    carry_scr[...] = jnp.broadcast_to(c[c.shape[0] - 1:, :], carry_scr.shape)


def _norm_gate(x, g, w_f, f_b, tm):
    s, d = x.shape
    return pl.pallas_call(
        _norm_gate_kernel,
        grid=(s // tm,),
        in_specs=[
            pl.BlockSpec((tm, d), lambda i: (i, 0)),
            pl.BlockSpec((1, d), lambda i: (0, 0)),
            pl.BlockSpec((LANES, d), lambda i: (0, 0)),
            pl.BlockSpec((1, LANES), lambda i: (0, 0)),
        ],
        out_specs=[pl.BlockSpec((tm // 2, d), lambda i: (i, 0)), pl.BlockSpec((tm, LANES), lambda i: (i, 0))],
        out_shape=[jax.ShapeDtypeStruct((s // 2, d), U32), jax.ShapeDtypeStruct((s, LANES), F32)],
        scratch_shapes=[pltpu.VMEM((SUBLANES, LANES), F32)],
        compiler_params=_params(1),
        name="norm_gate",
    )(x, g, w_f, f_b)


def _in_proj_kernel(h_hbm, wt_ref, o_hbm, w_scr, h_buf, o_buf, in_sem, out_sem, *, n_q_tiles, tm, tn, n_tiles):
    j = pl.program_id(0)
    w_scr[...] = wt_ref[...].astype(BF16)
    scale = jnp.where(j < n_q_tiles, HEAD_DIM ** -0.5 * LOG2E, 1.0).astype(F32)
    col0 = pl.multiple_of(j * tn, tn)

    def h_copy(i, slot):
        rows = pl.ds(pl.multiple_of(i * (tm // 2), tm // 2), tm // 2)
        return pltpu.make_async_copy(h_hbm.at[rows, :], h_buf.at[slot], in_sem.at[slot])

    def o_copy(i, slot):
        rows = pl.ds(pl.multiple_of(i * tm, tm), tm)
        return pltpu.make_async_copy(o_buf.at[slot], o_hbm.at[rows, pl.ds(col0, tn)], out_sem.at[slot])

    def tile(i, slot):
        @pl.when(i + 1 < n_tiles)
        def _():
            h_copy(i + 1, 1 - slot).start()

        h_copy(i, slot).wait()

        @pl.when(i >= 2)
        def _():
            o_copy(i - 2, slot).wait()

        o_buf[slot] = (_dot_nt(_unpack_rows(h_buf[slot]), w_scr[...]) * scale).astype(o_buf.dtype)
        o_copy(i, slot).start()

    h_copy(0, 0).start()

    def pair(t, carry):
        tile(2 * t, 0)
        tile(2 * t + 1, 1)
        return carry

    lax.fori_loop(0, n_tiles // 2, pair, 0)
    o_copy(n_tiles - 2, 0).wait()
    o_copy(n_tiles - 1, 1).wait()


def _in_proj(h, w_in_t, row0, out_dtype, tm, tn, name):
    s, d = 2 * h.shape[0], h.shape[1]
    n_tiles = s // tm
    assert n_tiles % 2 == 0 and n_tiles >= 2
    kern = functools.partial(_in_proj_kernel, n_q_tiles=ATT_WIDTH // tn, tm=tm, tn=tn, n_tiles=n_tiles)
    return pl.pallas_call(
        kern,
        grid=(3 * ATT_WIDTH // tn,),
        in_specs=[
            pl.BlockSpec(memory_space=pl.ANY),
            pl.BlockSpec((pl.Element(tn), pl.Element(d)),
                         lambda j: (pl.multiple_of(row0 + j * tn, SUBLANES), 0)),
        ],
        out_specs=pl.BlockSpec(memory_space=pl.ANY),
        out_shape=jax.ShapeDtypeStruct((s, 3 * ATT_WIDTH), out_dtype),
        scratch_shapes=[pltpu.VMEM((tn, d), BF16), pltpu.VMEM((2, tm // 2, d), U32),
                        pltpu.VMEM((2, tm, tn), out_dtype),
                        pltpu.SemaphoreType.DMA((2,)), pltpu.SemaphoreType.DMA((2,))],
        compiler_params=_params(1),
        name=name,
    )(h, w_in_t)


def _split_bf16x3(v):
    hi = lax.reduce_precision(v, exponent_bits=8, mantissa_bits=7)
    rest = v - hi
    mid = lax.reduce_precision(rest, exponent_bits=8, mantissa_bits=7)
    return hi, mid, rest - mid


def _fox_key_bias(c):
    ck = (c[:, :N_HEADS].T * (-LOG2E))[:, :, None]
    hi, mid, lo = _split_bf16x3(ck)
    lane = lax.broadcasted_iota(jnp.int32, (1, 1, HEAD_DIM), 2)
    cols = jnp.where(lane == 0, hi, jnp.where(lane == 1, mid, jnp.where(lane == 2, lo, 0.0)))
    return cols.astype(BF16)


def _fox_kernel(q_ref, k_ref, kc_ref, v_ref, o_ref, acc_scr, m_scr, *, tq, tk, ts):
    i = pl.program_id(1)
    lane = lax.broadcasted_iota(jnp.int32, (ts, HEAD_DIM), 1)
    bias_ones = (lane < N_BIAS_COLS).astype(BF16)
    ones = jnp.ones((tk, HEAD_DIM), BF16)
    m_scr[...] = jnp.full(m_scr.shape, -jnp.inf, F32)
    acc_scr[...] = jnp.zeros(acc_scr.shape, F32)

    def step(kb, diag):
        rows_k = pl.ds(pl.multiple_of(kb * tk, tk), tk)
        k_aug = jnp.concatenate([k_ref[rows_k, :], kc_ref[rows_k, :]], axis=1)
        v_aug = jnp.concatenate([v_ref[rows_k, :], ones], axis=1)
        for r0 in range(0, tq, ts):
            if diag is not None and r0 + ts - 1 < diag * tk:
                continue
            rs = slice(r0, r0 + ts)
            nk = tk if diag is None else min(tk, r0 + ts - diag * tk)
            q_aug = jnp.concatenate([q_ref[rs, :], bias_ones], axis=1)
            s = _dot_nt(q_aug, k_aug[:nk])
            if diag is not None and r0 < diag * tk + nk - 1:
                rows = lax.broadcasted_iota(jnp.int32, s.shape, 0)
                cols = lax.broadcasted_iota(jnp.int32, s.shape, 1)
                s = jnp.where(cols + (diag * tk - r0) <= rows, s, -jnp.inf)
            chunks = [s[:, c * LANES:(c + 1) * LANES] for c in range(nk // LANES)]
            m_cur = functools.reduce(jnp.maximum, chunks)
            m_prev = m_scr[rs, :]
            m_new = jnp.maximum(m_prev, jnp.max(m_cur, axis=1, keepdims=True))
            alpha = jnp.exp2(m_prev - m_new)
            p = jnp.concatenate([jnp.exp2(ch - m_new) for ch in chunks], axis=1).astype(BF16)
            acc_scr[rs, :] = acc_scr[rs, :] * jnp.concatenate([alpha, alpha], axis=1) + _dot(p, v_aug[:nk])
            m_scr[rs, :] = m_new

    def pair(t, carry):
        step(2 * t, None)
        step(2 * t + 1, None)
        return carry

    n_diag = tq // tk
    n_full = i * n_diag
    lax.fori_loop(0, n_full // 2, pair, 0)

    @pl.when(n_full % 2 == 1)
    def _():
        step(n_full - 1, None)

    for dg in range(n_diag):
        step(i * n_diag + dg, dg)
    acc = acc_scr[...]
    o_ref[...] = (acc[:, :HEAD_DIM] / acc[:, HEAD_DIM:]).astype(o_ref.dtype)


def _fox_attention(qkv, kc, tq, tk, ts):
    s = qkv.shape[0]
    kern = functools.partial(_fox_kernel, tq=tq, tk=tk, ts=ts)
    return pl.pallas_call(
        kern,
        grid=(N_HEADS, s // tq),
        in_specs=[
            pl.BlockSpec((tq, HEAD_DIM), lambda h, i: (i, h)),
            pl.BlockSpec((s, HEAD_DIM), lambda h, i: (0, N_HEADS + h)),
            pl.BlockSpec((None, s, HEAD_DIM), lambda h, i: (h, 0, 0)),
            pl.BlockSpec((s, HEAD_DIM), lambda h, i: (0, 2 * N_HEADS + h)),
        ],
        out_specs=pl.BlockSpec((tq, HEAD_DIM), lambda h, i: (i, h)),
        out_shape=jax.ShapeDtypeStruct((s, ATT_WIDTH), BF16),
        scratch_shapes=[pltpu.VMEM((tq, 2 * HEAD_DIM), F32), pltpu.VMEM((tq, HEAD_DIM), F32)],
        compiler_params=_params(2),
        name="fox_attention",
    )(qkv, qkv, kc, qkv)


def _t5_bucket_np(dist):
    max_exact = N_BUCKETS // 2
    d_f = np.maximum(dist, 1).astype(np.float32)
    ratio = np.log(d_f / np.float32(max_exact)) / np.float32(math.log(MAX_DISTANCE / max_exact))
    large = max_exact + (ratio * np.float32(N_BUCKETS - max_exact)).astype(np.int32)
    large = np.minimum(large, N_BUCKETS - 1)
    return np.where(dist < max_exact, dist, large).astype(np.int32)


def _dil_bias_tiles(rel_bias_table):
    n_valid = DIL_BLOCK + 1
    dist = np.stack([(DIL_BLOCK - np.arange(n_valid)) * dil for _, dil in DIL_PATTERNS])
    row = rel_bias_table.astype(F32)[_t5_bucket_np(dist)]
    row = jnp.transpose(row, (2, 0, 1)) * LOG2E
    row = jnp.concatenate([row, jnp.full(row.shape[:2] + (DIL_BLOCK,), -jnp.inf, F32)], axis=-1)
    flat = jnp.tile(row, (1, 1, DIL_BLOCK))[..., :DIL_BLOCK * 2 * DIL_BLOCK]
    return flat.reshape(row.shape[:2] + (DIL_BLOCK, 2 * DIL_BLOCK))


def _dil_block(q, k_prev, k_cur, v_prev, v_cur, bias):
    k2 = jnp.concatenate([k_prev, k_cur], axis=0)
    v2 = jnp.concatenate([v_prev, v_cur], axis=0)
    s = _dot_nt(q, k2) + bias
    m = jnp.max(s, axis=1, keepdims=True)
    p = jnp.exp2(s - m)
    den = jnp.sum(p, axis=1, keepdims=True)
    o = _dot(p.astype(BF16), v2) * (1.0 / den)
    return o, m + jnp.log(den) * LOG2E


def _dil_kernel(q_ref, kp_ref, kc_ref, vp_ref, vc_ref, bias_ref, o_ref,
                o1, o2, o3, l1, l2, l3):
    first = pl.program_id(1) == 0
    o_scr = (o1, o2, o3)
    l_scr = (l1, l2, l3)
    neg_inf = jnp.full((DIL_BLOCK, DIL_BLOCK), -jnp.inf, F32)

    for br, (_, dil) in enumerate(DIL_PATTERNS):
        n_blocks = DIL_SUPER // (DIL_BLOCK * dil)
        bias = bias_ref[br]
        bias_first = jnp.concatenate(
            [jnp.where(first, neg_inf, bias[:, :DIL_BLOCK]), bias[:, DIL_BLOCK:]], axis=1)

        def rows(m, r, dil=dil):
            if dil == 1:
                return pl.ds(m * DIL_BLOCK, DIL_BLOCK)
            return pl.ds(m * DIL_BLOCK * dil + r, DIL_BLOCK, stride=dil)

        def residue(r, carry, br=br, n_blocks=n_blocks, bias=bias, bias_first=bias_first, rows=rows):
            k_prev = kp_ref[rows(n_blocks - 1, r), :].astype(BF16)
            v_prev = vp_ref[rows(n_blocks - 1, r), :].astype(BF16)
            for m in range(n_blocks):
                idx = rows(m, r)
                q = q_ref[idx, :].astype(BF16)
                k_cur = kc_ref[idx, :].astype(BF16)
                v_cur = vc_ref[idx, :].astype(BF16)
                o, lse = _dil_block(q, k_prev, k_cur, v_prev, v_cur, bias_first if m == 0 else bias)
                o_scr[br][idx, :] = o
                l_scr[br][idx, :] = jnp.broadcast_to(lse, (DIL_BLOCK, HEAD_DIM))
                k_prev, v_prev = k_cur, v_cur
            return carry

        group = min(dil, DIL_RESIDUE_GROUP)

        def residue_group(g, carry, residue=residue, group=group):
            for r in range(group):
                residue(g * group + r, carry)
            return carry

        if dil == group:
            residue_group(0, 0)
        else:
            lax.fori_loop(0, dil // group, residue_group, 0)

    def merge(t, carry):
        idx = pl.ds(pl.multiple_of(t * DIL_BLOCK, DIL_BLOCK), DIL_BLOCK)
        la, lb, lc = l1[idx, :], l2[idx, :], l3[idx, :]
        mx = jnp.maximum(jnp.maximum(la, lb), lc)
        ea, eb, ec = jnp.exp2(la - mx), jnp.exp2(lb - mx), jnp.exp2(lc - mx)
        num = ea * o1[idx, :] + eb * o2[idx, :] + ec * o3[idx, :]
        o_ref[idx, :] = (num / (ea + eb + ec)).astype(o_ref.dtype)
        return carry

    lax.fori_loop(0, DIL_SUPER // DIL_BLOCK, merge, 0)


def _dilated_attention(qkv, bias_tiles):
    s = qkv.shape[0]
    blk = (DIL_SUPER, HEAD_DIM)
    prev = lambda off: (lambda h, b: (jnp.maximum(b - 1, 0), off + h))
    cur = lambda off: (lambda h, b: (b, off + h))
    scratch = [pltpu.VMEM(blk, F32) for _ in range(6)]
    return pl.pallas_call(
        _dil_kernel,
        grid=(N_HEADS, s // DIL_SUPER),
        in_specs=[
            pl.BlockSpec(blk, cur(0)),
            pl.BlockSpec(blk, prev(N_HEADS)),
            pl.BlockSpec(blk, cur(N_HEADS)),
            pl.BlockSpec(blk, prev(2 * N_HEADS)),
            pl.BlockSpec(blk, cur(2 * N_HEADS)),
            pl.BlockSpec((None, len(DIL_PATTERNS), DIL_BLOCK, 2 * DIL_BLOCK), lambda h, b: (h, 0, 0, 0)),
        ],
        out_specs=pl.BlockSpec(blk, lambda h, b: (b, h)),
        out_shape=jax.ShapeDtypeStruct((s, ATT_WIDTH), BF16),
        scratch_shapes=scratch,
        compiler_params=_params(2),
        name="dilated_attention",
    )(qkv, qkv, qkv, qkv, qkv, bias_tiles)


def _out_proj_kernel(a_ref, b_ref, ga_ref, gb_ref, w_ref, x_ref, gf_ref, o_ref, xg_ref, rs_ref, mix_scr,
                     *, d_model):
    j = pl.program_id(1)

    @pl.when(j == 0)
    def _():
        a = a_ref[...].astype(F32)
        b = b_ref[...].astype(F32)
        mix_scr[:, :ATT_WIDTH] = (a * _rms_scale(a) * ga_ref[...]).astype(BF16)
        mix_scr[:, ATT_WIDTH:] = (b * _rms_scale(b) * gb_ref[...]).astype(BF16)
        rs_ref[...] = jnp.zeros(rs_ref.shape, F32)

    x1 = x_ref[...] + _dot(mix_scr[...], w_ref[...])
    o_ref[...] = x1
    xg_ref[...] = _pack_rows((x1 * gf_ref[...]).astype(BF16))
    rs_ref[...] += jnp.sum(x1 * x1, axis=1, keepdims=True)

    @pl.when(j == pl.num_programs(1) - 1)
    def _():
        rs_ref[...] = lax.rsqrt(rs_ref[...] * (1.0 / d_model) + RMS_EPS)


def _out_proj(a, b, ga, gb, w, x, g_ffn, tm, tn):
    s, d = x.shape
    row = lambda i, j: (i, 0)
    tile = lambda i, j: (i, j)
    return pl.pallas_call(
        functools.partial(_out_proj_kernel, d_model=d),
        grid=(s // tm, d // tn),
        in_specs=[
            pl.BlockSpec((tm, ATT_WIDTH), row),
            pl.BlockSpec((tm, ATT_WIDTH), row),
            pl.BlockSpec((1, ATT_WIDTH), lambda i, j: (0, 0)),
            pl.BlockSpec((1, ATT_WIDTH), lambda i, j: (0, 0)),
            pl.BlockSpec((2 * ATT_WIDTH, tn), lambda i, j: (0, j)),
            pl.BlockSpec((tm, tn), tile),
            pl.BlockSpec((1, tn), lambda i, j: (0, j)),
        ],
        out_specs=[pl.BlockSpec((tm, tn), tile), pl.BlockSpec((tm // 2, tn), tile),
                   pl.BlockSpec((tm, LANES), row)],
        out_shape=[jax.ShapeDtypeStruct((s, d), F32), jax.ShapeDtypeStruct((s // 2, d), U32),
                   jax.ShapeDtypeStruct((s, LANES), F32)],
        scratch_shapes=[pltpu.VMEM((tm, 2 * ATT_WIDTH), BF16)],
        compiler_params=_params(2),
        name="out_proj",
    )(a, b, ga, gb, w, x, g_ffn)


def _ffn_up_kernel(xg_ref, rs_ref, wg_ref, wv_ref, cw_ref, cb_ref, o_ref, wg_scr, wv_scr, tail_scr,
                   *, rows_per_chunk):
    @pl.when(pl.program_id(1) == 0)
    def _():
        wg_scr[...] = wg_ref[...].astype(BF16)
        wv_scr[...] = wv_ref[...].astype(BF16)
        tail_scr[...] = jnp.zeros(tail_scr.shape, F32)

    tm, tn = rs_ref.shape[0], o_ref.shape[1]
    cw = cw_ref[...]
    cb = cb_ref[...]
    tail = tail_scr[...]
    sub = lax.broadcasted_iota(jnp.int32, tail.shape, 0)
    for r0 in range(0, tm, rows_per_chunk):
        rows = slice(r0, r0 + rows_per_chunk)
        packed_rows = slice(r0 // 2, (r0 + rows_per_chunk) // 2)
        xg = _unpack_rows(xg_ref[packed_rows, :])
        rs = jnp.concatenate([rs_ref[rows, :]] * (tn // LANES), axis=1)
        gate = _dot(xg, wg_scr[...]) * rs
        val = _dot(xg, wv_scr[...]) * rs

        def shifted(k, gate=gate, tail=tail):
            body = pltpu.roll(gate, k, axis=0)
            head = jnp.where(sub < k, pltpu.roll(tail, k, axis=0), body[:SUBLANES, :])
            return jnp.concatenate([head, body[SUBLANES:, :]], axis=0)

        conv = shifted(2) * cw[0:1, :] + shifted(1) * cw[1:2, :] + gate * cw[2:3, :] + cb
        act = conv * (1.0 / (1.0 + jnp.exp(-conv))) * val
        o_ref[packed_rows, :] = _pack_rows(act.astype(BF16))
        tail = gate[rows_per_chunk - SUBLANES:, :]
    tail_scr[...] = tail


def _ffn_up(xg, rs, w_up, conv_w, conv_b, tm, tn):
    s, d = rs.shape[0], xg.shape[1]
    f = conv_w.shape[1]
    n_f = f // tn
    return pl.pallas_call(
        functools.partial(_ffn_up_kernel, rows_per_chunk=min(128, tm)),
        grid=(n_f, s // tm),
        in_specs=[
            pl.BlockSpec((tm // 2, d), lambda j, i: (i, 0)),
            pl.BlockSpec((tm, LANES), lambda j, i: (i, 0)),
            pl.BlockSpec((d, tn), lambda j, i: (0, j)),
            pl.BlockSpec((d, tn), lambda j, i: (0, n_f + j)),
            pl.BlockSpec((CONV_WIDTH, tn), lambda j, i: (0, j)),
            pl.BlockSpec((1, tn), lambda j, i: (0, j)),
        ],
        out_specs=pl.BlockSpec((tm // 2, tn), lambda j, i: (i, j)),
        out_shape=jax.ShapeDtypeStruct((s // 2, f), U32),
        scratch_shapes=[pltpu.VMEM((d, tn), BF16), pltpu.VMEM((d, tn), BF16), pltpu.VMEM((SUBLANES, tn), F32)],
        compiler_params=_params(2),
        name="ffn_up",
    )(xg, rs, w_up, w_up, conv_w, conv_b)


def _ffn_down_kernel(a_ref, w_ref, x_ref, g_ref, o_ref, *, tn):
    n = pl.program_id(1)
    cols = pl.ds(pl.multiple_of(n * tn, tn), tn)
    o_ref[:, cols] = x_ref[...] + _dot(_unpack_rows(a_ref[...]), w_ref[...])

    @pl.when(n == pl.num_programs(1) - 1)
    def _():
        y = o_ref[...]
        o_ref[...] = y * _rms_scale(y) * g_ref[...]


def _ffn_down(act, w, x, g, tm, tn):
    s, d = x.shape
    f = act.shape[1]
    return pl.pallas_call(
        functools.partial(_ffn_down_kernel, tn=tn),
        grid=(s // tm, d // tn),
        in_specs=[
            pl.BlockSpec((tm // 2, f), lambda i, n: (i, 0)),
            pl.BlockSpec((f, tn), lambda i, n: (0, n)),
            pl.BlockSpec((tm, tn), lambda i, n: (i, n)),
            pl.BlockSpec((1, d), lambda i, n: (0, 0)),
        ],
        out_specs=pl.BlockSpec((tm, d), lambda i, n: (i, 0)),
        out_shape=jax.ShapeDtypeStruct((s, d), F32),
        compiler_params=_params(2, vmem_limit=60 * 1024 * 1024),
        name="ffn_down",
    )(act, w, x, g)


def _layer(x, attn_g, w_in, forget_b, bias_tiles, fox_g, dil_g, w_out, ffn_g, w_up, conv_w, conv_b,
           w_down, final_g):
    s, d = x.shape
    c0 = 3 * ATT_WIDTH
    c1 = c0 + N_HEADS
    tm_small = min(512, s)
    tm_big = min(1024, s)

    w_in_t = w_in.T
    w_f = jnp.pad(w_in_t[c0:c1], ((0, LANES - N_HEADS), (0, 0))).astype(BF16)
    f_b = jnp.pad(forget_b.astype(F32), (0, LANES - N_HEADS))[None, :]
    h, c = _norm_gate(x, attn_g[None, :], w_f, f_b, tm_small)

    qkv_a = _in_proj(h, w_in_t, 0, BF16, tm_big, 512, "in_proj_fox")
    qkv_b = _in_proj(h, w_in_t, c1, F32, tm_big, 512, "in_proj_dil")

    out_a = _fox_attention(qkv_a, _fox_key_bias(c), min(2048, s), tm_big, 256)
    out_b = _dilated_attention(qkv_b, bias_tiles)

    x1, xg, rs = _out_proj(out_a, out_b, fox_g[None, :], dil_g[None, :], w_out.astype(BF16), x,
                           ffn_g[None, :], tm_big, min(512, d))
    act = _ffn_up(xg, rs, w_up, conv_w, conv_b[None, :], tm_big, 256)
    return _ffn_down(act, w_down.astype(BF16), x1, final_g[None, :], tm_small, min(256, d))


def kernel(x, attn_norm_g, w_in, fox_forget_b, rel_bias_table, fox_out_norm_g, dil_out_norm_g, w_out,
           ffn_norm_g, w_up, conv_w, conv_b, w_down, final_norm_g):
    depth = w_in.shape[0]
    assert depth == 1 and x.shape[0] == 1, "single layer, single sequence"
    assert x.shape[1] % DIL_SUPER == 0 and conv_w.shape[-1] % 256 == 0
    bias_tiles = _dil_bias_tiles(rel_bias_table)
    y = _layer(x[0], attn_norm_g[0], w_in[0], fox_forget_b[0], bias_tiles, fox_out_norm_g[0],
               dil_out_norm_g[0], w_out[0], ffn_norm_g[0], w_up[0], conv_w[0], conv_b[0], w_down[0],
               final_norm_g)
    return y[None]
```

```python
import functools
import math

import jax
import jax.numpy as jnp
import numpy as np
from jax import lax
from jax.experimental import pallas as pl
from jax.experimental.pallas import tpu as pltpu

HEAD_DIM = 128
N_HEADS = 16
ATT_WIDTH = N_HEADS * HEAD_DIM
DIL_PATTERNS = ((128, 1), (512, 4), (2048, 16))
DIL_BLOCK = 128
DIL_SUPER = 2048
DIL_RESIDUE_GROUP = 16
N_BUCKETS = 32
MAX_DISTANCE = 2048
RMS_EPS = 1e-6
CONV_WIDTH = 3
LOG2E = math.log2(math.e)
N_BIAS_COLS = 3
LANES = 128
SUBLANES = 8
VMEM_LIMIT = 56 * 1024 * 1024

F32 = jnp.float32
BF16 = jnp.bfloat16
U32 = jnp.uint32


def _dot(a, b):
    return jnp.dot(a, b, preferred_element_type=F32)


def _dot_nt(a, b):
    return lax.dot_general(a, b, (((1,), (1,)), ((), ())), preferred_element_type=F32)


def _pack_rows(v):
    return pltpu.bitcast(v, U32)


def _unpack_rows(v):
    return pltpu.bitcast(v, BF16)


def _rms_scale(x):
    return lax.rsqrt(jnp.mean(x * x, axis=-1, keepdims=True) + RMS_EPS)


def _cumsum_rows(v):
    n = v.shape[0]
    rows = lax.broadcasted_iota(jnp.int32, v.shape, 0)
    s = 1
    while s < n:
        v = v + jnp.where(rows >= s, pltpu.roll(v, s, axis=0), 0.0)
        s *= 2
    return v


def _params(n_grid_dims, vmem_limit=VMEM_LIMIT):
    return pltpu.CompilerParams(dimension_semantics=("arbitrary",) * n_grid_dims,
                                vmem_limit_bytes=vmem_limit)


def _norm_gate_kernel(x_ref, g_ref, wf_ref, fb_ref, h_ref, c_ref, carry_scr):
    x = x_ref[...]
    hb = (x * _rms_scale(x) * g_ref[...]).astype(BF16)
    h_ref[...] = _pack_rows(hb)
    z = _dot_nt(hb, wf_ref[...]) + fb_ref[...]
    log_f = jnp.minimum(z, 0.0) - jnp.log1p(jnp.exp(-jnp.abs(z)))

    @pl.when(pl.program_id(0) == 0)
    def _():
        carry_scr[...] = jnp.zeros(carry_scr.shape, F32)

    c = _cumsum_rows(log_f) + carry_scr[0:1, :]
    c_ref[...] = c
    carry_scr[...] = jnp.broadcast_to(c[c.shape[0] - 1:, :], carry_scr.shape)


def _norm_gate(x, g, w_f, f_b, tm):
    s, d = x.shape
    return pl.pallas_call(
        _norm_gate_kernel,
        grid=(s // tm,),
        in_specs=[
            pl.BlockSpec((tm, d), lambda i: (i, 0)),
            pl.BlockSpec((1, d), lambda i: (0, 0)),
            pl.BlockSpec((LANES, d), lambda i: (0, 0)),
            pl.BlockSpec((1, LANES), lambda i: (0, 0)),
        ],
        out_specs=[pl.BlockSpec((tm // 2, d), lambda i: (i, 0)), pl.BlockSpec((tm, LANES), lambda i: (i, 0))],
        out_shape=[jax.ShapeDtypeStruct((s // 2, d), U32), jax.ShapeDtypeStruct((s, LANES), F32)],
        scratch_shapes=[pltpu.VMEM((SUBLANES, LANES), F32)],
        compiler_params=_params(1),
        name="norm_gate",
    )(x, g, w_f, f_b)


def _stage_weight_tile(j, n_j, copies, cast):
    @pl.when(pl.program_id(1) == 0)
    def _():
        @pl.when(j == 0)
        def _():
            for cp in copies(j):
                cp.start()

        for cp in copies(j):
            cp.wait()
        cast()

        @pl.when(j + 1 < n_j)
        def _():
            for cp in copies(j + 1):
                cp.start()


def _in_proj_kernel(h_ref, wt_hbm, o_ref, w_stage, w_scr, sem, *, row0):
    j = pl.program_id(0)
    tn = w_scr.shape[0]

    def copies(jj):
        rows = pl.ds(pl.multiple_of(row0 + jj * tn, SUBLANES), tn)
        return [pltpu.make_async_copy(wt_hbm.at[rows, :], w_stage, sem.at[0])]

    def cast():
        w_scr[...] = w_stage[...].astype(BF16)

    _stage_weight_tile(j, pl.num_programs(0), copies, cast)
    scale = jnp.where(j * tn < ATT_WIDTH, HEAD_DIM ** -0.5 * LOG2E, 1.0).astype(F32)
    o_ref[...] = (_dot_nt(_unpack_rows(h_ref[...]), w_scr[...]) * scale).astype(o_ref.dtype)


def _in_proj(h, w_in_t, row0, out_dtype, tm, tn, name):
    s, d = 2 * h.shape[0], h.shape[1]
    assert ATT_WIDTH % tn == 0
    kern = functools.partial(_in_proj_kernel, row0=row0)
    return pl.pallas_call(
        kern,
        grid=(3 * ATT_WIDTH // tn, s // tm),
        in_specs=[
            pl.BlockSpec((tm // 2, d), lambda j, i: (i, 0)),
            pl.BlockSpec(memory_space=pl.ANY),
        ],
        out_specs=pl.BlockSpec((tm, tn), lambda j, i: (i, j)),
        out_shape=jax.ShapeDtypeStruct((s, 3 * ATT_WIDTH), out_dtype),
        scratch_shapes=[pltpu.VMEM((tn, d), F32), pltpu.VMEM((tn, d), BF16), pltpu.SemaphoreType.DMA((1,))],
        compiler_params=_params(2),
        name=name,
    )(h, w_in_t)


def _split_bf16x3(v):
    hi = lax.reduce_precision(v, exponent_bits=8, mantissa_bits=7)
    rest = v - hi
    mid = lax.reduce_precision(rest, exponent_bits=8, mantissa_bits=7)
    return hi, mid, rest - mid


def _fox_key_bias(c):
    ck = (c[:, :N_HEADS].T * (-LOG2E))[:, :, None]
    hi, mid, lo = _split_bf16x3(ck)
    lane = lax.broadcasted_iota(jnp.int32, (1, 1, HEAD_DIM), 2)
    cols = jnp.where(lane == 0, hi, jnp.where(lane == 1, mid, jnp.where(lane == 2, lo, 0.0)))
    return cols.astype(BF16)


def _fox_kernel(q_ref, k_ref, kc_ref, v_ref, o_ref, acc_scr, m_scr, *, tq, tk, ts):
    i = pl.program_id(1)
    lane = lax.broadcasted_iota(jnp.int32, (ts, HEAD_DIM), 1)
    bias_ones = (lane < N_BIAS_COLS).astype(BF16)
    ones = jnp.ones((tk, HEAD_DIM), BF16)
    m_scr[...] = jnp.full(m_scr.shape, -jnp.inf, F32)
    acc_scr[...] = jnp.zeros(acc_scr.shape, F32)

    def step(kb, diag):
        rows_k = pl.ds(pl.multiple_of(kb * tk, tk), tk)
        k_aug = jnp.concatenate([k_ref[rows_k, :], kc_ref[rows_k, :]], axis=1)
        v_aug = jnp.concatenate([v_ref[rows_k, :], ones], axis=1)
        for r0 in range(0, tq, ts):
            if diag is not None and r0 + ts - 1 < diag * tk:
                continue
            rs = slice(r0, r0 + ts)
            nk = tk if diag is None else min(tk, r0 + ts - diag * tk)
            q_aug = jnp.concatenate([q_ref[rs, :], bias_ones], axis=1)
            s = _dot_nt(q_aug, k_aug[:nk])
            if diag is not None and r0 < diag * tk + nk - 1:
                rows = lax.broadcasted_iota(jnp.int32, s.shape, 0)
                cols = lax.broadcasted_iota(jnp.int32, s.shape, 1)
                s = jnp.where(cols + (diag * tk - r0) <= rows, s, -jnp.inf)
            chunks = [s[:, c * LANES:(c + 1) * LANES] for c in range(nk // LANES)]
            m_cur = functools.reduce(jnp.maximum, chunks)
            m_prev = m_scr[rs, :]
            m_new = jnp.maximum(m_prev, jnp.max(m_cur, axis=1, keepdims=True))
            alpha = jnp.exp2(m_prev - m_new)
            p = jnp.concatenate([jnp.exp2(ch - m_new) for ch in chunks], axis=1).astype(BF16)
            acc_scr[rs, :] = acc_scr[rs, :] * jnp.concatenate([alpha, alpha], axis=1) + _dot(p, v_aug[:nk])
            m_scr[rs, :] = m_new

    def pair(t, carry):
        step(2 * t, None)
        step(2 * t + 1, None)
        return carry

    n_diag = tq // tk
    n_full = i * n_diag
    lax.fori_loop(0, n_full // 2, pair, 0)

    @pl.when(n_full % 2 == 1)
    def _():
        step(n_full - 1, None)

    for dg in range(n_diag):
        step(i * n_diag + dg, dg)
    acc = acc_scr[...]
    o_ref[...] = (acc[:, :HEAD_DIM] / acc[:, HEAD_DIM:]).astype(o_ref.dtype)


def _fox_attention(qkv, kc, tq, tk, ts):
    s = qkv.shape[0]
    kern = functools.partial(_fox_kernel, tq=tq, tk=tk, ts=ts)
    return pl.pallas_call(
        kern,
        grid=(N_HEADS, s // tq),
        in_specs=[
            pl.BlockSpec((tq, HEAD_DIM), lambda h, i: (i, h)),
            pl.BlockSpec((s, HEAD_DIM), lambda h, i: (0, N_HEADS + h)),
            pl.BlockSpec((None, s, HEAD_DIM), lambda h, i: (h, 0, 0)),
            pl.BlockSpec((s, HEAD_DIM), lambda h, i: (0, 2 * N_HEADS + h)),
        ],
        out_specs=pl.BlockSpec((tq, HEAD_DIM), lambda h, i: (i, h)),
        out_shape=jax.ShapeDtypeStruct((s, ATT_WIDTH), BF16),
        scratch_shapes=[pltpu.VMEM((tq, 2 * HEAD_DIM), F32), pltpu.VMEM((tq, HEAD_DIM), F32)],
        compiler_params=_params(2),
        name="fox_attention",
    )(qkv, qkv, kc, qkv)


def _t5_bucket_np(dist):
    max_exact = N_BUCKETS // 2
    d_f = np.maximum(dist, 1).astype(np.float32)
    ratio = np.log(d_f / np.float32(max_exact)) / np.float32(math.log(MAX_DISTANCE / max_exact))
    large = max_exact + (ratio * np.float32(N_BUCKETS - max_exact)).astype(np.int32)
    large = np.minimum(large, N_BUCKETS - 1)
    return np.where(dist < max_exact, dist, large).astype(np.int32)


def _dil_bias_tiles(rel_bias_table):
    n_valid = DIL_BLOCK + 1
    dist = np.stack([(DIL_BLOCK - np.arange(n_valid)) * dil for _, dil in DIL_PATTERNS])
    row = rel_bias_table.astype(F32)[_t5_bucket_np(dist)]
    row = jnp.transpose(row, (2, 0, 1)) * LOG2E
    row = jnp.concatenate([row, jnp.full(row.shape[:2] + (DIL_BLOCK,), -jnp.inf, F32)], axis=-1)
    flat = jnp.tile(row, (1, 1, DIL_BLOCK))[..., :DIL_BLOCK * 2 * DIL_BLOCK]
    return flat.reshape(row.shape[:2] + (DIL_BLOCK, 2 * DIL_BLOCK))


def _dil_block(q, k_prev, k_cur, v_prev, v_cur, bias):
    k2 = jnp.concatenate([k_prev, k_cur], axis=0)
    v2 = jnp.concatenate([v_prev, v_cur], axis=0)
    s = _dot_nt(q, k2) + bias
    m = jnp.max(s, axis=1, keepdims=True)
    p = jnp.exp2(s - m)
    den = jnp.sum(p, axis=1, keepdims=True)
    o = _dot(p.astype(BF16), v2) * (1.0 / den)
    return o, m + jnp.log(den) * LOG2E


def _dil_kernel(q_ref, kp_ref, kc_ref, vp_ref, vc_ref, bias_ref, o_ref,
                o1, o2, o3, l1, l2, l3):
    first = pl.program_id(1) == 0
    o_scr = (o1, o2, o3)
    l_scr = (l1, l2, l3)
    neg_inf = jnp.full((DIL_BLOCK, DIL_BLOCK), -jnp.inf, F32)

    for br, (_, dil) in enumerate(DIL_PATTERNS):
        n_blocks = DIL_SUPER // (DIL_BLOCK * dil)
        bias = bias_ref[br]
        bias_first = jnp.concatenate(
            [jnp.where(first, neg_inf, bias[:, :DIL_BLOCK]), bias[:, DIL_BLOCK:]], axis=1)

        def rows(m, r, dil=dil):
            if dil == 1:
                return pl.ds(m * DIL_BLOCK, DIL_BLOCK)
            return pl.ds(m * DIL_BLOCK * dil + r, DIL_BLOCK, stride=dil)

        def residue(r, carry, br=br, n_blocks=n_blocks, bias=bias, bias_first=bias_first, rows=rows):
            k_prev = kp_ref[rows(n_blocks - 1, r), :].astype(BF16)
            v_prev = vp_ref[rows(n_blocks - 1, r), :].astype(BF16)
            for m in range(n_blocks):
                idx = rows(m, r)
                q = q_ref[idx, :].astype(BF16)
                k_cur = kc_ref[idx, :].astype(BF16)
                v_cur = vc_ref[idx, :].astype(BF16)
                o, lse = _dil_block(q, k_prev, k_cur, v_prev, v_cur, bias_first if m == 0 else bias)
                o_scr[br][idx, :] = o
                l_scr[br][idx, :] = jnp.broadcast_to(lse, (DIL_BLOCK, HEAD_DIM))
                k_prev, v_prev = k_cur, v_cur
            return carry

        group = min(dil, DIL_RESIDUE_GROUP)

        def residue_group(g, carry, residue=residue, group=group):
            for r in range(group):
                residue(g * group + r, carry)
            return carry

        if dil == group:
            residue_group(0, 0)
        else:
            lax.fori_loop(0, dil // group, residue_group, 0)

    def merge(t, carry):
        idx = pl.ds(pl.multiple_of(t * DIL_BLOCK, DIL_BLOCK), DIL_BLOCK)
        la, lb, lc = l1[idx, :], l2[idx, :], l3[idx, :]
        mx = jnp.maximum(jnp.maximum(la, lb), lc)
        ea, eb, ec = jnp.exp2(la - mx), jnp.exp2(lb - mx), jnp.exp2(lc - mx)
        num = ea * o1[idx, :] + eb * o2[idx, :] + ec * o3[idx, :]
        o_ref[idx, :] = (num / (ea + eb + ec)).astype(o_ref.dtype)
        return carry

    lax.fori_loop(0, DIL_SUPER // DIL_BLOCK, merge, 0)


def _dilated_attention(qkv, bias_tiles):
    s = qkv.shape[0]
    blk = (DIL_SUPER, HEAD_DIM)
    prev = lambda off: (lambda h, b: (jnp.maximum(b - 1, 0), off + h))
    cur = lambda off: (lambda h, b: (b, off + h))
    scratch = [pltpu.VMEM(blk, F32) for _ in range(6)]
    return pl.pallas_call(
        _dil_kernel,
        grid=(N_HEADS, s // DIL_SUPER),
        in_specs=[
            pl.BlockSpec(blk, cur(0)),
            pl.BlockSpec(blk, prev(N_HEADS)),
            pl.BlockSpec(blk, cur(N_HEADS)),
            pl.BlockSpec(blk, prev(2 * N_HEADS)),
            pl.BlockSpec(blk, cur(2 * N_HEADS)),
            pl.BlockSpec((None, len(DIL_PATTERNS), DIL_BLOCK, 2 * DIL_BLOCK), lambda h, b: (h, 0, 0, 0)),
        ],
        out_specs=pl.BlockSpec(blk, lambda h, b: (b, h)),
        out_shape=jax.ShapeDtypeStruct((s, ATT_WIDTH), BF16),
        scratch_shapes=scratch,
        compiler_params=_params(2),
        name="dilated_attention",
    )(qkv, qkv, qkv, qkv, qkv, bias_tiles)


def _out_proj_kernel(a_ref, b_ref, ga_ref, gb_ref, w_ref, x_ref, gf_ref, o_ref, xg_ref, rs_ref, mix_scr,
                     *, d_model):
    j = pl.program_id(1)

    @pl.when(j == 0)
    def _():
        a = a_ref[...].astype(F32)
        b = b_ref[...].astype(F32)
        mix_scr[:, :ATT_WIDTH] = (a * _rms_scale(a) * ga_ref[...]).astype(BF16)
        mix_scr[:, ATT_WIDTH:] = (b * _rms_scale(b) * gb_ref[...]).astype(BF16)
        rs_ref[...] = jnp.zeros(rs_ref.shape, F32)

    x1 = x_ref[...] + _dot(mix_scr[...], w_ref[...])
    o_ref[...] = x1
    xg_ref[...] = _pack_rows((x1 * gf_ref[...]).astype(BF16))
    rs_ref[...] += jnp.sum(x1 * x1, axis=1, keepdims=True)

    @pl.when(j == pl.num_programs(1) - 1)
    def _():
        rs_ref[...] = lax.rsqrt(rs_ref[...] * (1.0 / d_model) + RMS_EPS)


def _out_proj(a, b, ga, gb, w, x, g_ffn, tm, tn):
    s, d = x.shape
    row = lambda i, j: (i, 0)
    tile = lambda i, j: (i, j)
    return pl.pallas_call(
        functools.partial(_out_proj_kernel, d_model=d),
        grid=(s // tm, d // tn),
        in_specs=[
            pl.BlockSpec((tm, ATT_WIDTH), row),
            pl.BlockSpec((tm, ATT_WIDTH), row),
            pl.BlockSpec((1, ATT_WIDTH), lambda i, j: (0, 0)),
            pl.BlockSpec((1, ATT_WIDTH), lambda i, j: (0, 0)),
            pl.BlockSpec((2 * ATT_WIDTH, tn), lambda i, j: (0, j)),
            pl.BlockSpec((tm, tn), tile),
            pl.BlockSpec((1, tn), lambda i, j: (0, j)),
        ],
        out_specs=[pl.BlockSpec((tm, tn), tile), pl.BlockSpec((tm // 2, tn), tile),
                   pl.BlockSpec((tm, LANES), row)],
        out_shape=[jax.ShapeDtypeStruct((s, d), F32), jax.ShapeDtypeStruct((s // 2, d), U32),
                   jax.ShapeDtypeStruct((s, LANES), F32)],
        scratch_shapes=[pltpu.VMEM((tm, 2 * ATT_WIDTH), BF16)],
        compiler_params=_params(2),
        name="out_proj",
    )(a, b, ga, gb, w, x, g_ffn)


def _ffn_tile_start(j, tn, n_features):
    return pl.multiple_of(jnp.minimum(j * tn, n_features - tn), LANES)


def _ffn_up_kernel(xg_ref, rs_ref, w_hbm, cw_ref, cb_ref, o_ref, wg_stage, wv_stage, wg_scr, wv_scr, tail_scr,
                   sem, *, rows_per_chunk, n_features, last_shift):
    tm, tn = rs_ref.shape[0], o_ref.shape[1]

    def copies(jj):
        c0 = _ffn_tile_start(jj, tn, n_features)
        return [pltpu.make_async_copy(w_hbm.at[:, pl.ds(c0, tn)], wg_stage, sem.at[0]),
                pltpu.make_async_copy(w_hbm.at[:, pl.ds(n_features + c0, tn)], wv_stage, sem.at[1])]

    def cast():
        wg_scr[...] = wg_stage[...].astype(BF16)
        wv_scr[...] = wv_stage[...].astype(BF16)

    _stage_weight_tile(pl.program_id(0), pl.num_programs(0), copies, cast)

    @pl.when(pl.program_id(1) == 0)
    def _():
        tail_scr[...] = jnp.zeros(tail_scr.shape, F32)

    cw = cw_ref[...]
    cb = cb_ref[...]
    tail = tail_scr[...]
    sub = lax.broadcasted_iota(jnp.int32, tail.shape, 0)
    for r0 in range(0, tm, rows_per_chunk):
        rows = slice(r0, r0 + rows_per_chunk)
        packed_rows = slice(r0 // 2, (r0 + rows_per_chunk) // 2)
        xg = _unpack_rows(xg_ref[packed_rows, :])
        rs = jnp.concatenate([rs_ref[rows, :]] * (tn // LANES), axis=1)
        gate = _dot(xg, wg_scr[...]) * rs
        val = _dot(xg, wv_scr[...]) * rs

        def shifted(k, gate=gate, tail=tail):
            body = pltpu.roll(gate, k, axis=0)
            head = jnp.where(sub < k, pltpu.roll(tail, k, axis=0), body[:SUBLANES, :])
            return jnp.concatenate([head, body[SUBLANES:, :]], axis=0)

        conv = shifted(2) * cw[0:1, :] + shifted(1) * cw[1:2, :] + gate * cw[2:3, :] + cb
        act = conv * (1.0 / (1.0 + jnp.exp(-conv))) * val
        if last_shift:
            moved = jnp.concatenate([act[:, last_shift:], act[:, :last_shift]], axis=1)
            act = jnp.where(pl.program_id(0) == pl.num_programs(0) - 1, moved, act)
        o_ref[packed_rows, :] = _pack_rows(act.astype(BF16))
        tail = gate[rows_per_chunk - SUBLANES:, :]
    tail_scr[...] = tail


def _ffn_up(xg, rs, w_up, conv_w, conv_b, tm, tn):
    s, d = rs.shape[0], xg.shape[1]
    f = conv_w.shape[1]
    assert f % LANES == 0 and f >= tn
    n_f = -(-f // tn)
    start = lambda j: _ffn_tile_start(j, tn, f)
    return pl.pallas_call(
        functools.partial(_ffn_up_kernel, rows_per_chunk=min(128, tm), n_features=f, last_shift=n_f * tn - f),
        grid=(n_f, s // tm),
        in_specs=[
            pl.BlockSpec((tm // 2, d), lambda j, i: (i, 0)),
            pl.BlockSpec((tm, LANES), lambda j, i: (i, 0)),
            pl.BlockSpec(memory_space=pl.ANY),
            pl.BlockSpec((pl.Element(CONV_WIDTH), pl.Element(tn)), lambda j, i: (0, start(j))),
            pl.BlockSpec((pl.Element(1), pl.Element(tn)), lambda j, i: (0, start(j))),
        ],
        out_specs=pl.BlockSpec((tm // 2, tn), lambda j, i: (i, j)),
        out_shape=jax.ShapeDtypeStruct((s // 2, n_f * tn), U32),
        scratch_shapes=[pltpu.VMEM((d, tn), F32), pltpu.VMEM((d, tn), F32),
                        pltpu.VMEM((d, tn), BF16), pltpu.VMEM((d, tn), BF16),
                        pltpu.VMEM((SUBLANES, tn), F32), pltpu.SemaphoreType.DMA((2,))],
        compiler_params=_params(2),
        name="ffn_up",
    )(xg, rs, w_up, conv_w, conv_b)


def _ffn_down_kernel(a_ref, w_ref, x_ref, g_ref, o_ref, *, tn):
    n = pl.program_id(1)
    cols = pl.ds(pl.multiple_of(n * tn, tn), tn)
    o_ref[:, cols] = x_ref[...] + _dot(_unpack_rows(a_ref[...]), w_ref[...])

    @pl.when(n == pl.num_programs(1) - 1)
    def _():
        y = o_ref[...]
        o_ref[...] = y * _rms_scale(y) * g_ref[...]


def _ffn_down(act, w, x, g, tm, tn):
    s, d = x.shape
    f = w.shape[0]
    return pl.pallas_call(
        functools.partial(_ffn_down_kernel, tn=tn),
        grid=(s // tm, d // tn),
        in_specs=[
            pl.BlockSpec((tm // 2, f), lambda i, n: (i, 0)),
            pl.BlockSpec((f, tn), lambda i, n: (0, n)),
            pl.BlockSpec((tm, tn), lambda i, n: (i, n)),
            pl.BlockSpec((1, d), lambda i, n: (0, 0)),
        ],
        out_specs=pl.BlockSpec((tm, d), lambda i, n: (i, 0)),
        out_shape=jax.ShapeDtypeStruct((s, d), F32),
        compiler_params=_params(2, vmem_limit=60 * 1024 * 1024),
        name="ffn_down",
    )(act, w, x, g)


def _layer(x, attn_g, w_in, forget_b, bias_tiles, fox_g, dil_g, w_out, ffn_g, w_up, conv_w, conv_b,
           w_down, final_g):
    s, d = x.shape
    c0 = 3 * ATT_WIDTH
    c1 = c0 + N_HEADS
    tm_small = min(512, s)
    tm_big = min(1024, s)

    w_in_t = w_in.T
    w_f = jnp.pad(w_in_t[c0:c1], ((0, LANES - N_HEADS), (0, 0))).astype(BF16)
    f_b = jnp.pad(forget_b.astype(F32), (0, LANES - N_HEADS))[None, :]
    h, c = _norm_gate(x, attn_g[None, :], w_f, f_b, tm_small)

    qkv_a = _in_proj(h, w_in_t, 0, BF16, tm_big, 1024, "in_proj_fox")
    qkv_b = _in_proj(h, w_in_t, c1, F32, tm_big, 1024, "in_proj_dil")

    out_a = _fox_attention(qkv_a, _fox_key_bias(c), min(2048, s), tm_big, 256)
    out_b = _dilated_attention(qkv_b, bias_tiles)

    x1, xg, rs = _out_proj(out_a, out_b, fox_g[None, :], dil_g[None, :], w_out.astype(BF16), x,
                           ffn_g[None, :], tm_big, min(512, d))
    act = _ffn_up(xg, rs, w_up, conv_w, conv_b[None, :], tm_big, min(512, conv_w.shape[1]))
    return _ffn_down(act, w_down.astype(BF16), x1, final_g[None, :], tm_small, min(256, d))


def kernel(x, attn_norm_g, w_in, fox_forget_b, rel_bias_table, fox_out_norm_g, dil_out_norm_g, w_out,
           ffn_norm_g, w_up, conv_w, conv_b, w_down, final_norm_g):
    depth = w_in.shape[0]
    assert depth == 1 and x.shape[0] == 1, "single layer, single sequence"
    assert x.shape[1] % DIL_SUPER == 0 and conv_w.shape[-1] % 256 == 0
    bias_tiles = _dil_bias_tiles(rel_bias_table)
    y = _layer(x[0], attn_norm_g[0], w_in[0], fox_forget_b[0], bias_tiles, fox_out_norm_g[0],
               dil_out_norm_g[0], w_out[0], ffn_norm_g[0], w_up[0], conv_w[0], conv_b[0], w_down[0],
               final_norm_g)
    return y[None]
```

```python
import functools
import math

import jax
import jax.numpy as jnp
import numpy as np
from jax import lax
from jax.experimental import pallas as pl
from jax.experimental.pallas import tpu as pltpu

HEAD_DIM = 128
N_HEADS = 16
ATT_WIDTH = N_HEADS * HEAD_DIM
DIL_PATTERNS = ((128, 1), (512, 4), (2048, 16))
DIL_BLOCK = 128
DIL_SUPER = 2048
DIL_DEINT = 4
N_BUCKETS = 32
MAX_DISTANCE = 2048
RMS_EPS = 1e-6
CONV_WIDTH = 3
LOG2E = math.log2(math.e)
N_BIAS_COLS = 3
LANES = 128
SUBLANES = 8
VMEM_LIMIT = 56 * 1024 * 1024

F32 = jnp.float32
BF16 = jnp.bfloat16
U32 = jnp.uint32


def _dot(a, b):
    return jnp.dot(a, b, preferred_element_type=F32)


def _dot_nt(a, b):
    return lax.dot_general(a, b, (((1,), (1,)), ((), ())), preferred_element_type=F32)


def _pack_rows(v):
    return pltpu.bitcast(v, U32)


def _unpack_rows(v):
    return pltpu.bitcast(v, BF16)


def _rms_scale(x):
    return lax.rsqrt(jnp.mean(x * x, axis=-1, keepdims=True) + RMS_EPS)


def _cumsum_rows(v):
    n = v.shape[0]
    rows = lax.broadcasted_iota(jnp.int32, v.shape, 0)
    s = 1
    while s < n:
        v = v + jnp.where(rows >= s, pltpu.roll(v, s, axis=0), 0.0)
        s *= 2
    return v


def _params(n_grid_dims, vmem_limit=VMEM_LIMIT):
    return pltpu.CompilerParams(dimension_semantics=("arbitrary",) * n_grid_dims,
                                vmem_limit_bytes=vmem_limit)


def _truncate_to_bf16(v):
    bits = lax.bitcast_convert_type(v, U32) & jnp.uint32(0xFFFF0000)
    return lax.bitcast_convert_type(bits, F32)


def _norm_gate_kernel(x_ref, g_ref, wf_ref, fb_ref, h_ref, kc_ref, carry_scr):
    x = x_ref[...]
    hb = (x * _rms_scale(x) * g_ref[...]).astype(BF16)
    h_ref[...] = _pack_rows(hb)
    z = _dot_nt(hb, wf_ref[...]) + fb_ref[...]
    log_f = jnp.minimum(z, 0.0) - jnp.log1p(jnp.exp(-jnp.abs(z)))

    @pl.when(pl.program_id(0) == 0)
    def _():
        carry_scr[...] = jnp.zeros(carry_scr.shape, F32)

    c = _cumsum_rows(log_f) + carry_scr[0:1, :]
    carry_scr[...] = jnp.broadcast_to(c[c.shape[0] - 1:, :], carry_scr.shape)

    ck = c * (-LOG2E)
    hi = _truncate_to_bf16(ck)
    mid = _truncate_to_bf16(ck - hi)
    lo = ck - hi - mid
    lane = lax.broadcasted_iota(jnp.int32, c.shape, 1)
    for head in range(N_HEADS):
        col = slice(head, head + 1)
        cols = jnp.where(lane == 0, hi[:, col], jnp.where(lane == 1, mid[:, col], jnp.where(lane == 2, lo[:, col], 0.0)))
        kc_ref[head] = cols.astype(BF16)


def _norm_gate(x, g, w_f, f_b, tm):
    s, d = x.shape
    return pl.pallas_call(
        _norm_gate_kernel,
        grid=(s // tm,),
        in_specs=[
            pl.BlockSpec((tm, d), lambda i: (i, 0)),
            pl.BlockSpec((1, d), lambda i: (0, 0)),
            pl.BlockSpec((LANES, d), lambda i: (0, 0)),
            pl.BlockSpec((1, LANES), lambda i: (0, 0)),
        ],
        out_specs=[pl.BlockSpec((tm // 2, d), lambda i: (i, 0)),
                   pl.BlockSpec((N_HEADS, tm, HEAD_DIM), lambda i: (0, i, 0))],
        out_shape=[jax.ShapeDtypeStruct((s // 2, d), U32), jax.ShapeDtypeStruct((N_HEADS, s, HEAD_DIM), BF16)],
        scratch_shapes=[pltpu.VMEM((SUBLANES, LANES), F32)],
        compiler_params=_params(1),
        name="norm_gate",
    )(x, g, w_f, f_b)


def _stage_weight_tile(j, n_j, copies, cast):
    @pl.when(pl.program_id(1) == 0)
    def _():
        @pl.when(j == 0)
        def _():
            for cp in copies(j):
                cp.start()

        for cp in copies(j):
            cp.wait()
        cast()

        @pl.when(j + 1 < n_j)
        def _():
            for cp in copies(j + 1):
                cp.start()


def _in_proj_kernel(h_ref, wt_hbm, o_ref, w_stage, w_scr, sem, *, row0):
    j = pl.program_id(0)
    tn = w_scr.shape[0]

    def copies(jj):
        rows = pl.ds(pl.multiple_of(row0 + jj * tn, SUBLANES), tn)
        return [pltpu.make_async_copy(wt_hbm.at[rows, :], w_stage, sem.at[0])]

    def cast():
        w_scr[...] = w_stage[...].astype(BF16)

    _stage_weight_tile(j, pl.num_programs(0), copies, cast)
    scale = jnp.where(j * tn < ATT_WIDTH, HEAD_DIM ** -0.5 * LOG2E, 1.0).astype(F32)
    o_ref[...] = (_dot_nt(_unpack_rows(h_ref[...]), w_scr[...]) * scale).astype(o_ref.dtype)


def _in_proj(h, w_in_t, row0, out_dtype, tm, tn, name):
    s, d = 2 * h.shape[0], h.shape[1]
    assert ATT_WIDTH % tn == 0
    kern = functools.partial(_in_proj_kernel, row0=row0)
    return pl.pallas_call(
        kern,
        grid=(3 * ATT_WIDTH // tn, s // tm),
        in_specs=[
            pl.BlockSpec((tm // 2, d), lambda j, i: (i, 0)),
            pl.BlockSpec(memory_space=pl.ANY),
        ],
        out_specs=pl.BlockSpec((tm, tn), lambda j, i: (i, j)),
        out_shape=jax.ShapeDtypeStruct((s, 3 * ATT_WIDTH), out_dtype),
        scratch_shapes=[pltpu.VMEM((tn, d), F32), pltpu.VMEM((tn, d), BF16), pltpu.SemaphoreType.DMA((1,))],
        compiler_params=_params(2),
        name=name,
    )(h, w_in_t)


def _fox_kernel(q_ref, k_ref, kc_ref, v_ref, o_ref, acc_scr, m_scr, *, tq, tk, ts):
    i = pl.program_id(1)
    lane = lax.broadcasted_iota(jnp.int32, (ts, HEAD_DIM), 1)
    bias_ones = (lane < N_BIAS_COLS).astype(BF16)
    ones = jnp.ones((tk, HEAD_DIM), BF16)
    m_scr[...] = jnp.full(m_scr.shape, -jnp.inf, F32)
    acc_scr[...] = jnp.zeros(acc_scr.shape, F32)

    def step(kb, diag):
        rows_k = pl.ds(pl.multiple_of(kb * tk, tk), tk)
        k_aug = jnp.concatenate([k_ref[rows_k, :], kc_ref[rows_k, :]], axis=1)
        v_aug = jnp.concatenate([v_ref[rows_k, :], ones], axis=1)
        for r0 in range(0, tq, ts):
            if diag is not None and r0 + ts - 1 < diag * tk:
                continue
            rs = slice(r0, r0 + ts)
            nk = tk if diag is None else min(tk, r0 + ts - diag * tk)
            q_aug = jnp.concatenate([q_ref[rs, :], bias_ones], axis=1)
            s = _dot_nt(q_aug, k_aug[:nk])
            if diag is not None and r0 < diag * tk + nk - 1:
                rows = lax.broadcasted_iota(jnp.int32, s.shape, 0)
                cols = lax.broadcasted_iota(jnp.int32, s.shape, 1)
                s = jnp.where(cols + (diag * tk - r0) <= rows, s, -jnp.inf)
            chunks = [s[:, c * LANES:(c + 1) * LANES] for c in range(nk // LANES)]
            m_cur = functools.reduce(jnp.maximum, chunks)
            m_prev = m_scr[rs, :]
            m_new = jnp.maximum(m_prev, jnp.max(m_cur, axis=1, keepdims=True))
            alpha = jnp.exp2(m_prev - m_new)
            p = jnp.concatenate([jnp.exp2(ch - m_new) for ch in chunks], axis=1).astype(BF16)
            acc_scr[rs, :] = acc_scr[rs, :] * jnp.concatenate([alpha, alpha], axis=1) + _dot(p, v_aug[:nk])
            m_scr[rs, :] = m_new

    def pair(t, carry):
        step(2 * t, None)
        step(2 * t + 1, None)
        return carry

    n_diag = tq // tk
    n_full = i * n_diag
    lax.fori_loop(0, n_full // 2, pair, 0)

    @pl.when(n_full % 2 == 1)
    def _():
        step(n_full - 1, None)

    for dg in range(n_diag):
        step(i * n_diag + dg, dg)
    acc = acc_scr[...]
    o_ref[...] = (acc[:, :HEAD_DIM] / acc[:, HEAD_DIM:]).astype(o_ref.dtype)


def _fox_attention(qkv, kc, tq, tk, ts):
    s = qkv.shape[0]
    kern = functools.partial(_fox_kernel, tq=tq, tk=tk, ts=ts)
    return pl.pallas_call(
        kern,
        grid=(N_HEADS, s // tq),
        in_specs=[
            pl.BlockSpec((tq, HEAD_DIM), lambda h, i: (i, h)),
            pl.BlockSpec((s, HEAD_DIM), lambda h, i: (0, N_HEADS + h)),
            pl.BlockSpec((None, s, HEAD_DIM), lambda h, i: (h, 0, 0)),
            pl.BlockSpec((s, HEAD_DIM), lambda h, i: (0, 2 * N_HEADS + h)),
        ],
        out_specs=pl.BlockSpec((tq, HEAD_DIM), lambda h, i: (i, h)),
        out_shape=jax.ShapeDtypeStruct((s, ATT_WIDTH), BF16),
        scratch_shapes=[pltpu.VMEM((tq, 2 * HEAD_DIM), F32), pltpu.VMEM((tq, HEAD_DIM), F32)],
        compiler_params=_params(2),
        name="fox_attention",
    )(qkv, qkv, kc, qkv)


def _t5_bucket_np(dist):
    max_exact = N_BUCKETS // 2
    d_f = np.maximum(dist, 1).astype(np.float32)
    ratio = np.log(d_f / np.float32(max_exact)) / np.float32(math.log(MAX_DISTANCE / max_exact))
    large = max_exact + (ratio * np.float32(N_BUCKETS - max_exact)).astype(np.int32)
    large = np.minimum(large, N_BUCKETS - 1)
    return np.where(dist < max_exact, dist, large).astype(np.int32)


def _dil_bias_tiles(rel_bias_table):
    n_valid = DIL_BLOCK + 1
    dist = np.stack([(DIL_BLOCK - np.arange(n_valid)) * dil for _, dil in DIL_PATTERNS])
    row = rel_bias_table.astype(F32)[_t5_bucket_np(dist)]
    row = jnp.transpose(row, (2, 0, 1)) * LOG2E
    row = jnp.concatenate([row, jnp.full(row.shape[:2] + (DIL_BLOCK,), -jnp.inf, F32)], axis=-1)
    flat = jnp.tile(row, (1, 1, DIL_BLOCK))[..., :DIL_BLOCK * 2 * DIL_BLOCK]
    return flat.reshape(row.shape[:2] + (DIL_BLOCK, 2 * DIL_BLOCK))


def _dil_block(q, k_prev, k_cur, v_prev, v_cur, bias):
    k2 = jnp.concatenate([k_prev, k_cur], axis=0)
    v2 = jnp.concatenate([v_prev, v_cur], axis=0)
    s = _dot_nt(q, k2) + bias
    m = jnp.max(s, axis=1, keepdims=True)
    p = jnp.exp2(s - m)
    den = jnp.sum(p, axis=1, keepdims=True)
    o = _dot(p.astype(BF16), v2) * (1.0 / den)
    return o, m + jnp.log(den) * LOG2E


def _dil_kernel(q_ref, k_ref, v_ref, kp_ref, vp_ref, bias_ref, o_ref,
                q4, k4, v4, k4p, v4p, o1, l1, o4, l4, o16, l16, out_f):
    first = pl.program_id(1) == 0
    neg_inf = jnp.full((DIL_BLOCK, DIL_BLOCK), -jnp.inf, F32)
    n_sub = DIL_SUPER // DIL_DEINT
    bf = lambda t: t.astype(BF16)
    rep = lambda lse: jnp.broadcast_to(lse, (DIL_BLOCK, HEAD_DIM))

    for a in range(DIL_DEINT):
        idx = pl.ds(a, n_sub, stride=DIL_DEINT)
        q4[a] = q_ref[idx, :]
        k4[a] = k_ref[idx, :]
        v4[a] = v_ref[idx, :]

    @pl.when(first)
    def _():
        k4p[...] = jnp.zeros(k4p.shape, F32)
        v4p[...] = jnp.zeros(v4p.shape, F32)

    def biases(br):
        bias = bias_ref[br]
        bias_first = jnp.concatenate(
            [jnp.where(first, neg_inf, bias[:, :DIL_BLOCK]), bias[:, DIL_BLOCK:]], axis=1)
        return bias, bias_first

    bias, bias_first = biases(0)
    k_prev, v_prev = bf(kp_ref[...]), bf(vp_ref[...])
    for m in range(DIL_SUPER // DIL_BLOCK):
        idx = pl.ds(m * DIL_BLOCK, DIL_BLOCK)
        k_cur, v_cur = bf(k_ref[idx, :]), bf(v_ref[idx, :])
        o, lse = _dil_block(bf(q_ref[idx, :]), k_prev, k_cur, v_prev, v_cur, bias_first if m == 0 else bias)
        o1[idx, :] = o
        l1[idx, :] = rep(lse)
        k_prev, v_prev = k_cur, v_cur

    bias, bias_first = biases(1)
    for a in range(DIL_DEINT):
        last = pl.ds(n_sub - DIL_BLOCK, DIL_BLOCK)
        k_prev, v_prev = bf(k4p[a, last, :]), bf(v4p[a, last, :])
        for m in range(n_sub // DIL_BLOCK):
            idx = pl.ds(m * DIL_BLOCK, DIL_BLOCK)
            k_cur, v_cur = bf(k4[a, idx, :]), bf(v4[a, idx, :])
            o, lse = _dil_block(bf(q4[a, idx, :]), k_prev, k_cur, v_prev, v_cur, bias_first if m == 0 else bias)
            o4[a, idx, :] = o
            l4[a, idx, :] = rep(lse)
            k_prev, v_prev = k_cur, v_cur

    _, bias_first = biases(2)
    for a in range(DIL_DEINT):
        for r in range(DIL_DEINT):
            idx = pl.ds(r, DIL_BLOCK, stride=DIL_DEINT)
            o, lse = _dil_block(bf(q4.at[a][idx, :]), bf(k4p.at[a][idx, :]), bf(k4.at[a][idx, :]),
                                bf(v4p.at[a][idx, :]), bf(v4.at[a][idx, :]), bias_first)
            o16.at[a][idx, :] = o
            l16.at[a][idx, :] = rep(lse)

    k4p[...] = k4[...]
    v4p[...] = v4[...]

    for a in range(DIL_DEINT):
        def merge(c, carry, a=a):
            idx = pl.ds(pl.multiple_of(c * DIL_BLOCK, DIL_BLOCK), DIL_BLOCK)
            nat = pl.ds(c * (DIL_BLOCK * DIL_DEINT) + a, DIL_BLOCK, stride=DIL_DEINT)
            la, lb, lc = l1[nat, :], l4[a, idx, :], l16[a, idx, :]
            mx = jnp.maximum(jnp.maximum(la, lb), lc)
            ea, eb, ec = jnp.exp2(la - mx), jnp.exp2(lb - mx), jnp.exp2(lc - mx)
            num = ea * o1[nat, :] + eb * o4[a, idx, :] + ec * o16[a, idx, :]
            out_f[nat, :] = num / (ea + eb + ec)
            return carry

        lax.fori_loop(0, n_sub // DIL_BLOCK, merge, 0)

    def emit(t, carry):
        idx = pl.ds(pl.multiple_of(t * DIL_BLOCK, DIL_BLOCK), DIL_BLOCK)
        o_ref[idx, :] = out_f[idx, :].astype(o_ref.dtype)
        return carry

    lax.fori_loop(0, DIL_SUPER // DIL_BLOCK, emit, 0)


def _dilated_attention(qkv, bias_tiles):
    assert tuple(d for _, d in DIL_PATTERNS) == (1, DIL_DEINT, DIL_DEINT ** 2)
    s = qkv.shape[0]
    blk = (DIL_SUPER, HEAD_DIM)
    tail = (DIL_BLOCK, HEAD_DIM)
    n_tail = DIL_SUPER // DIL_BLOCK
    cur = lambda off: (lambda h, b: (b, off + h))
    prev_tail = lambda off: (lambda h, b: (jnp.maximum(b * n_tail - 1, 0), off + h))
    cls = (DIL_DEINT, DIL_SUPER // DIL_DEINT, HEAD_DIM)
    scratch = ([pltpu.VMEM(cls, F32)] * 5
               + [pltpu.VMEM(blk, F32)] * 2 + [pltpu.VMEM(cls, F32)] * 4
               + [pltpu.VMEM(blk, F32)])
    return pl.pallas_call(
        _dil_kernel,
        grid=(N_HEADS, s // DIL_SUPER),
        in_specs=[
            pl.BlockSpec(blk, cur(0)),
            pl.BlockSpec(blk, cur(N_HEADS)),
            pl.BlockSpec(blk, cur(2 * N_HEADS)),
            pl.BlockSpec(tail, prev_tail(N_HEADS)),
            pl.BlockSpec(tail, prev_tail(2 * N_HEADS)),
            pl.BlockSpec((None, len(DIL_PATTERNS), DIL_BLOCK, 2 * DIL_BLOCK), lambda h, b: (h, 0, 0, 0)),
        ],
        out_specs=pl.BlockSpec(blk, lambda h, b: (b, h)),
        out_shape=jax.ShapeDtypeStruct((s, ATT_WIDTH), BF16),
        scratch_shapes=scratch,
        compiler_params=_params(2),
        name="dilated_attention",
    )(qkv, qkv, qkv, qkv, qkv, bias_tiles)


def _out_proj_kernel(a_ref, b_ref, ga_ref, gb_ref, w_ref, x_ref, gf_ref, o_ref, xg_ref, rs_ref, mix_scr,
                     *, d_model):
    j = pl.program_id(1)

    @pl.when(j == 0)
    def _():
        a = a_ref[...].astype(F32)
        b = b_ref[...].astype(F32)
        mix_scr[:, :ATT_WIDTH] = (a * _rms_scale(a) * ga_ref[...]).astype(BF16)
        mix_scr[:, ATT_WIDTH:] = (b * _rms_scale(b) * gb_ref[...]).astype(BF16)
        rs_ref[...] = jnp.zeros(rs_ref.shape, F32)

    x1 = x_ref[...] + _dot(mix_scr[...], w_ref[...])
    o_ref[...] = x1
    xg_ref[...] = _pack_rows((x1 * gf_ref[...]).astype(BF16))
    rs_ref[...] += jnp.sum(x1 * x1, axis=1, keepdims=True)

    @pl.when(j == pl.num_programs(1) - 1)
    def _():
        rs_ref[...] = lax.rsqrt(rs_ref[...] * (1.0 / d_model) + RMS_EPS)


def _out_proj(a, b, ga, gb, w, x, g_ffn, tm, tn):
    s, d = x.shape
    row = lambda i, j: (i, 0)
    tile = lambda i, j: (i, j)
    return pl.pallas_call(
        functools.partial(_out_proj_kernel, d_model=d),
        grid=(s // tm, d // tn),
        in_specs=[
            pl.BlockSpec((tm, ATT_WIDTH), row),
            pl.BlockSpec((tm, ATT_WIDTH), row),
            pl.BlockSpec((1, ATT_WIDTH), lambda i, j: (0, 0)),
            pl.BlockSpec((1, ATT_WIDTH), lambda i, j: (0, 0)),
            pl.BlockSpec((2 * ATT_WIDTH, tn), lambda i, j: (0, j)),
            pl.BlockSpec((tm, tn), tile),
            pl.BlockSpec((1, tn), lambda i, j: (0, j)),
        ],
        out_specs=[pl.BlockSpec((tm, tn), tile), pl.BlockSpec((tm // 2, tn), tile),
                   pl.BlockSpec((tm, LANES), row)],
        out_shape=[jax.ShapeDtypeStruct((s, d), F32), jax.ShapeDtypeStruct((s // 2, d), U32),
                   jax.ShapeDtypeStruct((s, LANES), F32)],
        scratch_shapes=[pltpu.VMEM((tm, 2 * ATT_WIDTH), BF16)],
        compiler_params=_params(2),
        name="out_proj",
    )(a, b, ga, gb, w, x, g_ffn)


def _ffn_tile_start(j, tn, n_features):
    return pl.multiple_of(jnp.minimum(j * tn, n_features - tn), LANES)


def _ffn_up_kernel(xg_ref, rs_ref, w_hbm, cw_ref, cb_ref, o_ref, wg_stage, wv_stage, wg_scr, wv_scr, tail_scr,
                   sem, *, rows_per_chunk, n_features, last_shift):
    tm, tn = rs_ref.shape[0], o_ref.shape[1]

    def copies(jj):
        c0 = _ffn_tile_start(jj, tn, n_features)
        return [pltpu.make_async_copy(w_hbm.at[:, pl.ds(c0, tn)], wg_stage, sem.at[0]),
                pltpu.make_async_copy(w_hbm.at[:, pl.ds(n_features + c0, tn)], wv_stage, sem.at[1])]

    def cast():
        wg_scr[...] = wg_stage[...].astype(BF16)
        wv_scr[...] = wv_stage[...].astype(BF16)

    _stage_weight_tile(pl.program_id(0), pl.num_programs(0), copies, cast)

    @pl.when(pl.program_id(1) == 0)
    def _():
        tail_scr[...] = jnp.zeros(tail_scr.shape, F32)

    cw = cw_ref[...]
    cb = cb_ref[...]
    tail = tail_scr[...]
    sub = lax.broadcasted_iota(jnp.int32, tail.shape, 0)
    for r0 in range(0, tm, rows_per_chunk):
        rows = slice(r0, r0 + rows_per_chunk)
        packed_rows = slice(r0 // 2, (r0 + rows_per_chunk) // 2)
        xg = _unpack_rows(xg_ref[packed_rows, :])
        rs = jnp.concatenate([rs_ref[rows, :]] * (tn // LANES), axis=1)
        gate = _dot(xg, wg_scr[...]) * rs
        val = _dot(xg, wv_scr[...]) * rs

        def shifted(k, gate=gate, tail=tail):
            body = pltpu.roll(gate, k, axis=0)
            head = jnp.where(sub < k, pltpu.roll(tail, k, axis=0), body[:SUBLANES, :])
            return jnp.concatenate([head, body[SUBLANES:, :]], axis=0)

        conv = shifted(2) * cw[0:1, :] + shifted(1) * cw[1:2, :] + gate * cw[2:3, :] + cb
        act = conv * (1.0 / (1.0 + jnp.exp(-conv))) * val
        if last_shift:
            moved = jnp.concatenate([act[:, last_shift:], act[:, :last_shift]], axis=1)
            act = jnp.where(pl.program_id(0) == pl.num_programs(0) - 1, moved, act)
        o_ref[packed_rows, :] = _pack_rows(act.astype(BF16))
        tail = gate[rows_per_chunk - SUBLANES:, :]
    tail_scr[...] = tail


def _ffn_up(xg, rs, w_up, conv_w, conv_b, tm, tn):
    s, d = rs.shape[0], xg.shape[1]
    f = conv_w.shape[1]
    assert f % LANES == 0 and f >= tn
    n_f = -(-f // tn)
    start = lambda j: _ffn_tile_start(j, tn, f)
    return pl.pallas_call(
        functools.partial(_ffn_up_kernel, rows_per_chunk=min(128, tm), n_features=f, last_shift=n_f * tn - f),
        grid=(n_f, s // tm),
        in_specs=[
            pl.BlockSpec((tm // 2, d), lambda j, i: (i, 0)),
            pl.BlockSpec((tm, LANES), lambda j, i: (i, 0)),
            pl.BlockSpec(memory_space=pl.ANY),
            pl.BlockSpec((pl.Element(CONV_WIDTH), pl.Element(tn)), lambda j, i: (0, start(j))),
            pl.BlockSpec((pl.Element(1), pl.Element(tn)), lambda j, i: (0, start(j))),
        ],
        out_specs=pl.BlockSpec((tm // 2, tn), lambda j, i: (i, j)),
        out_shape=jax.ShapeDtypeStruct((s // 2, n_f * tn), U32),
        scratch_shapes=[pltpu.VMEM((d, tn), F32), pltpu.VMEM((d, tn), F32),
                        pltpu.VMEM((d, tn), BF16), pltpu.VMEM((d, tn), BF16),
                        pltpu.VMEM((SUBLANES, tn), F32), pltpu.SemaphoreType.DMA((2,))],
        compiler_params=_params(2),
        name="ffn_up",
    )(xg, rs, w_up, conv_w, conv_b)


def _ffn_down_kernel(a_ref, w_ref, x_ref, g_ref, o_ref, *, tn):
    n = pl.program_id(1)
    cols = pl.ds(pl.multiple_of(n * tn, tn), tn)
    o_ref[:, cols] = x_ref[...] + _dot(_unpack_rows(a_ref[...]), w_ref[...])

    @pl.when(n == pl.num_programs(1) - 1)
    def _():
        y = o_ref[...]
        o_ref[...] = y * _rms_scale(y) * g_ref[...]


def _ffn_down(act, w, x, g, tm, tn):
    s, d = x.shape
    f = w.shape[0]
    return pl.pallas_call(
        functools.partial(_ffn_down_kernel, tn=tn),
        grid=(s // tm, d // tn),
        in_specs=[
            pl.BlockSpec((tm // 2, f), lambda i, n: (i, 0)),
            pl.BlockSpec((f, tn), lambda i, n: (0, n)),
            pl.BlockSpec((tm, tn), lambda i, n: (i, n)),
            pl.BlockSpec((1, d), lambda i, n: (0, 0)),
        ],
        out_specs=pl.BlockSpec((tm, d), lambda i, n: (i, 0)),
        out_shape=jax.ShapeDtypeStruct((s, d), F32),
        compiler_params=_params(2, vmem_limit=60 * 1024 * 1024),
        name="ffn_down",
    )(act, w, x, g)


def _layer(x, attn_g, w_in, forget_b, bias_tiles, fox_g, dil_g, w_out, ffn_g, w_up, conv_w, conv_b,
           w_down, final_g):
    s, d = x.shape
    c0 = 3 * ATT_WIDTH
    c1 = c0 + N_HEADS
    tm_small = min(512, s)
    tm_big = min(1024, s)

    w_in_t = w_in.T
    w_f = jnp.pad(w_in_t[c0:c1], ((0, LANES - N_HEADS), (0, 0))).astype(BF16)
    f_b = jnp.pad(forget_b.astype(F32), (0, LANES - N_HEADS))[None, :]
    h, kc = _norm_gate(x, attn_g[None, :], w_f, f_b, tm_small)

    qkv_a = _in_proj(h, w_in_t, 0, BF16, tm_big, 1024, "in_proj_fox")
    qkv_b = _in_proj(h, w_in_t, c1, F32, tm_big, 1024, "in_proj_dil")

    out_a = _fox_attention(qkv_a, kc, min(2048, s), tm_big, 256)
    out_b = _dilated_attention(qkv_b, bias_tiles)

    x1, xg, rs = _out_proj(out_a, out_b, fox_g[None, :], dil_g[None, :], w_out.astype(BF16), x,
                           ffn_g[None, :], tm_big, min(512, d))
    act = _ffn_up(xg, rs, w_up, conv_w, conv_b[None, :], tm_big, min(512, conv_w.shape[1]))
    return _ffn_down(act, w_down.astype(BF16), x1, final_g[None, :], tm_small, min(256, d))


def kernel(x, attn_norm_g, w_in, fox_forget_b, rel_bias_table, fox_out_norm_g, dil_out_norm_g, w_out,
           ffn_norm_g, w_up, conv_w, conv_b, w_down, final_norm_g):
    depth = w_in.shape[0]
    assert depth == 1 and x.shape[0] == 1, "single layer, single sequence"
    assert x.shape[1] % DIL_SUPER == 0 and conv_w.shape[-1] % 256 == 0
    bias_tiles = _dil_bias_tiles(rel_bias_table)
    y = _layer(x[0], attn_norm_g[0], w_in[0], fox_forget_b[0], bias_tiles, fox_out_norm_g[0],
               dil_out_norm_g[0], w_out[0], ffn_norm_g[0], w_up[0], conv_w[0], conv_b[0], w_down[0],
               final_norm_g)
    return y[None]
```

```python
import functools
import math

import jax
import jax.numpy as jnp
import numpy as np
from jax import lax
from jax.experimental import pallas as pl
from jax.experimental.pallas import tpu as pltpu

HEAD_DIM = 128
N_HEADS = 16
ATT_WIDTH = N_HEADS * HEAD_DIM
DIL_PATTERNS = ((128, 1), (512, 4), (2048, 16))
DIL_BLOCK = 128
DIL_SUPER = 2048
DIL_DEINT = 4
N_BUCKETS = 32
MAX_DISTANCE = 2048
RMS_EPS = 1e-6
CONV_WIDTH = 3
LOG2E = math.log2(math.e)
N_BIAS_COLS = 3
LANES = 128
SUBLANES = 8
VMEM_LIMIT = 56 * 1024 * 1024

F32 = jnp.float32
BF16 = jnp.bfloat16
U32 = jnp.uint32


def _dot(a, b):
    return jnp.dot(a, b, preferred_element_type=F32)


def _dot_nt(a, b):
    return lax.dot_general(a, b, (((1,), (1,)), ((), ())), preferred_element_type=F32)


def _pack_rows(v):
    return pltpu.bitcast(v, U32)


def _unpack_rows(v):
    return pltpu.bitcast(v, BF16)


def _rms_scale(x):
    return lax.rsqrt(jnp.mean(x * x, axis=-1, keepdims=True) + RMS_EPS)


def _cumsum_rows(v):
    n = v.shape[0]
    rows = lax.broadcasted_iota(jnp.int32, v.shape, 0)
    s = 1
    while s < n:
        v = v + jnp.where(rows >= s, pltpu.roll(v, s, axis=0), 0.0)
        s *= 2
    return v


def _params(n_grid_dims, vmem_limit=VMEM_LIMIT):
    return pltpu.CompilerParams(dimension_semantics=("arbitrary",) * n_grid_dims,
                                vmem_limit_bytes=vmem_limit)


def _truncate_to_bf16(v):
    bits = lax.bitcast_convert_type(v, U32) & jnp.uint32(0xFFFF0000)
    return lax.bitcast_convert_type(bits, F32)


def _norm_gate_kernel(x_ref, g_ref, wf_ref, fb_ref, h_ref, kc_ref, carry_scr):
    x = x_ref[...]
    hb = (x * _rms_scale(x) * g_ref[...]).astype(BF16)
    h_ref[...] = _pack_rows(hb)
    z = _dot_nt(hb, wf_ref[...]) + fb_ref[...]
    log_f = jnp.minimum(z, 0.0) - jnp.log1p(jnp.exp(-jnp.abs(z)))

    @pl.when(pl.program_id(0) == 0)
    def _():
        carry_scr[...] = jnp.zeros(carry_scr.shape, F32)

    c = _cumsum_rows(log_f) + carry_scr[0:1, :]
    carry_scr[...] = jnp.broadcast_to(c[c.shape[0] - 1:, :], carry_scr.shape)

    ck = c * (-LOG2E)
    hi = _truncate_to_bf16(ck)
    mid = _truncate_to_bf16(ck - hi)
    lo = ck - hi - mid
    lane = lax.broadcasted_iota(jnp.int32, c.shape, 1)
    for head in range(N_HEADS):
        col = slice(head, head + 1)
        cols = jnp.where(lane == 0, hi[:, col], jnp.where(lane == 1, mid[:, col], jnp.where(lane == 2, lo[:, col], 0.0)))
        kc_ref[head] = cols.astype(BF16)


def _norm_gate(x, g, w_f, f_b, tm):
    s, d = x.shape
    return pl.pallas_call(
        _norm_gate_kernel,
        grid=(s // tm,),
        in_specs=[
            pl.BlockSpec((tm, d), lambda i: (i, 0)),
            pl.BlockSpec((1, d), lambda i: (0, 0)),
            pl.BlockSpec((LANES, d), lambda i: (0, 0)),
            pl.BlockSpec((1, LANES), lambda i: (0, 0)),
        ],
        out_specs=[pl.BlockSpec((tm // 2, d), lambda i: (i, 0)),
                   pl.BlockSpec((N_HEADS, tm, HEAD_DIM), lambda i: (0, i, 0))],
        out_shape=[jax.ShapeDtypeStruct((s // 2, d), U32), jax.ShapeDtypeStruct((N_HEADS, s, HEAD_DIM), BF16)],
        scratch_shapes=[pltpu.VMEM((SUBLANES, LANES), F32)],
        compiler_params=_params(1),
        name="norm_gate",
    )(x, g, w_f, f_b)


def _stage_weight_tile(j, n_j, copies, cast):
    @pl.when(pl.program_id(1) == 0)
    def _():
        @pl.when(j == 0)
        def _():
            for cp in copies(j):
                cp.start()

        for cp in copies(j):
            cp.wait()
        cast()

        @pl.when(j + 1 < n_j)
        def _():
            for cp in copies(j + 1):
                cp.start()


def _in_proj_kernel(h_ref, wt_hbm, o_ref, w_stage, w_scr, sem, *, row0):
    j = pl.program_id(0)
    tn = w_scr.shape[0]

    def copies(jj):
        rows = pl.ds(pl.multiple_of(row0 + jj * tn, SUBLANES), tn)
        return [pltpu.make_async_copy(wt_hbm.at[rows, :], w_stage, sem.at[0])]

    def cast():
        w_scr[...] = w_stage[...].astype(BF16)

    _stage_weight_tile(j, pl.num_programs(0), copies, cast)
    scale = jnp.where(j * tn < ATT_WIDTH, HEAD_DIM ** -0.5 * LOG2E, 1.0).astype(F32)
    o_ref[...] = (_dot_nt(_unpack_rows(h_ref[...]), w_scr[...]) * scale).astype(o_ref.dtype)


def _in_proj(h, w_in_t, row0, out_dtype, tm, tn, name):
    s, d = 2 * h.shape[0], h.shape[1]
    assert ATT_WIDTH % tn == 0
    kern = functools.partial(_in_proj_kernel, row0=row0)
    return pl.pallas_call(
        kern,
        grid=(3 * ATT_WIDTH // tn, s // tm),
        in_specs=[
            pl.BlockSpec((tm // 2, d), lambda j, i: (i, 0)),
            pl.BlockSpec(memory_space=pl.ANY),
        ],
        out_specs=pl.BlockSpec((tm, tn), lambda j, i: (i, j)),
        out_shape=jax.ShapeDtypeStruct((s, 3 * ATT_WIDTH), out_dtype),
        scratch_shapes=[pltpu.VMEM((tn, d), F32), pltpu.VMEM((tn, d), BF16), pltpu.SemaphoreType.DMA((1,))],
        compiler_params=_params(2),
        name=name,
    )(h, w_in_t)


def _side_cast_rows(n_rows, n_steps):
    rows = -(-n_rows // n_steps)
    rows += -rows % (2 * SUBLANES)
    while n_rows % rows:
        rows += 2 * SUBLANES
    return rows


def _fox_kernel(q_ref, k_ref, kc_ref, v_ref, *rest, tq, tk, ts, side_blocks):
    n_side = len(side_blocks)
    side_src, (o_ref, *side_dst), (acc_scr, m_scr) = rest[:n_side], rest[n_side:2 * n_side + 1], rest[2 * n_side + 1:]
    step_id = pl.program_id(0) * pl.num_programs(1) + pl.program_id(1)
    for src, dst, n_blocks in zip(side_src, side_dst, side_blocks):
        @pl.when(step_id < n_blocks)
        def _(src=src, dst=dst):
            dst[...] = src[...].astype(dst.dtype)

    i = pl.program_id(1)
    lane = lax.broadcasted_iota(jnp.int32, (ts, HEAD_DIM), 1)
    bias_ones = (lane < N_BIAS_COLS).astype(BF16)
    ones = jnp.ones((tk, HEAD_DIM), BF16)
    m_scr[...] = jnp.full(m_scr.shape, -jnp.inf, F32)
    acc_scr[...] = jnp.zeros(acc_scr.shape, F32)

    def step(kb, diag):
        rows_k = pl.ds(pl.multiple_of(kb * tk, tk), tk)
        k_aug = jnp.concatenate([k_ref[rows_k, :], kc_ref[rows_k, :]], axis=1)
        v_aug = jnp.concatenate([v_ref[rows_k, :], ones], axis=1)
        for r0 in range(0, tq, ts):
            if diag is not None and r0 + ts - 1 < diag * tk:
                continue
            rs = slice(r0, r0 + ts)
            nk = tk if diag is None else min(tk, r0 + ts - diag * tk)
            q_aug = jnp.concatenate([q_ref[rs, :], bias_ones], axis=1)
            s = _dot_nt(q_aug, k_aug[:nk])
            if diag is not None and r0 < diag * tk + nk - 1:
                rows = lax.broadcasted_iota(jnp.int32, s.shape, 0)
                cols = lax.broadcasted_iota(jnp.int32, s.shape, 1)
                s = jnp.where(cols + (diag * tk - r0) <= rows, s, -jnp.inf)
            chunks = [s[:, c * LANES:(c + 1) * LANES] for c in range(nk // LANES)]
            m_cur = functools.reduce(jnp.maximum, chunks)
            m_prev = m_scr[rs, :]
            m_new = jnp.maximum(m_prev, jnp.max(m_cur, axis=1, keepdims=True))
            alpha = jnp.exp2(m_prev - m_new)
            p = jnp.concatenate([jnp.exp2(ch - m_new) for ch in chunks], axis=1).astype(BF16)
            acc_scr[rs, :] = acc_scr[rs, :] * jnp.concatenate([alpha, alpha], axis=1) + _dot(p, v_aug[:nk])
            m_scr[rs, :] = m_new

    def pair(t, carry):
        step(2 * t, None)
        step(2 * t + 1, None)
        return carry

    n_diag = tq // tk
    n_pairs = i * (n_diag // 2)

    def diagonal():
        for dg in range(n_diag):
            step(i * n_diag + dg, dg)

    @pl.when(i == 0)
    def _():
        diagonal()

    @pl.when(i > 0)
    def _():
        lax.fori_loop(0, n_pairs - 1, pair, 0)
        pair(n_pairs - 1, 0)
        diagonal()

    acc = acc_scr[...]
    o_ref[...] = (acc[:, :HEAD_DIM] / acc[:, HEAD_DIM:]).astype(o_ref.dtype)


def _fox_attention(qkv, kc, side_weights, tq, tk, ts):
    s = qkv.shape[0]
    assert (tq // tk) % 2 == 0 and tq % ts == 0
    n_i = s // tq
    n_steps = N_HEADS * n_i
    side_specs, side_blocks = [], []
    for w in side_weights:
        rows = _side_cast_rows(w.shape[0], n_steps)
        n_blocks = w.shape[0] // rows
        side_blocks.append(n_blocks)
        side_specs.append(pl.BlockSpec(
            (rows, w.shape[1]), lambda h, i, n_blocks=n_blocks: (jnp.minimum(h * n_i + i, n_blocks - 1), 0)))
    kern = functools.partial(_fox_kernel, tq=tq, tk=tk, ts=ts, side_blocks=tuple(side_blocks))
    return pl.pallas_call(
        kern,
        grid=(N_HEADS, n_i),
        in_specs=[
            pl.BlockSpec((tq, HEAD_DIM), lambda h, i: (i, h)),
            pl.BlockSpec((s, HEAD_DIM), lambda h, i: (0, N_HEADS + h)),
            pl.BlockSpec((None, s, HEAD_DIM), lambda h, i: (h, 0, 0)),
            pl.BlockSpec((s, HEAD_DIM), lambda h, i: (0, 2 * N_HEADS + h)),
        ] + side_specs,
        out_specs=[pl.BlockSpec((tq, HEAD_DIM), lambda h, i: (i, h))] + side_specs,
        out_shape=[jax.ShapeDtypeStruct((s, ATT_WIDTH), BF16)]
        + [jax.ShapeDtypeStruct(w.shape, BF16) for w in side_weights],
        scratch_shapes=[pltpu.VMEM((tq, 2 * HEAD_DIM), F32), pltpu.VMEM((tq, HEAD_DIM), F32)],
        compiler_params=_params(2),
        name="fox_attention",
    )(qkv, qkv, kc, qkv, *side_weights)


def _t5_bucket_np(dist):
    max_exact = N_BUCKETS // 2
    d_f = np.maximum(dist, 1).astype(np.float32)
    ratio = np.log(d_f / np.float32(max_exact)) / np.float32(math.log(MAX_DISTANCE / max_exact))
    large = max_exact + (ratio * np.float32(N_BUCKETS - max_exact)).astype(np.int32)
    large = np.minimum(large, N_BUCKETS - 1)
    return np.where(dist < max_exact, dist, large).astype(np.int32)


def _dil_bias_tiles(rel_bias_table):
    n_valid = DIL_BLOCK + 1
    dist = np.stack([(DIL_BLOCK - np.arange(n_valid)) * dil for _, dil in DIL_PATTERNS])
    row = rel_bias_table.astype(F32)[_t5_bucket_np(dist)]
    row = jnp.transpose(row, (2, 0, 1)) * LOG2E
    row = jnp.concatenate([row, jnp.full(row.shape[:2] + (DIL_BLOCK,), -jnp.inf, F32)], axis=-1)
    flat = jnp.tile(row, (1, 1, DIL_BLOCK))[..., :DIL_BLOCK * 2 * DIL_BLOCK]
    return flat.reshape(row.shape[:2] + (DIL_BLOCK, 2 * DIL_BLOCK))


def _dil_block(q, k_prev, k_cur, v_prev, v_cur, bias):
    k2 = jnp.concatenate([k_prev, k_cur], axis=0)
    v2 = jnp.concatenate([v_prev, v_cur], axis=0)
    s = _dot_nt(q, k2) + bias
    m = jnp.max(s, axis=1, keepdims=True)
    p = jnp.exp2(s - m)
    den = jnp.sum(p, axis=1, keepdims=True)
    o = _dot(p.astype(BF16), v2) * (1.0 / den)
    return o, m + jnp.log(den) * LOG2E


def _dil_kernel(q_ref, k_ref, v_ref, kp_ref, vp_ref, bias_ref, o_ref,
                q4, k4, v4, k4p, v4p, o1, l1, o4, l4, o16, l16, out_f):
    first = pl.program_id(1) == 0
    neg_inf = jnp.full((DIL_BLOCK, DIL_BLOCK), -jnp.inf, F32)
    n_sub = DIL_SUPER // DIL_DEINT
    bf = lambda t: t.astype(BF16)
    rep = lambda lse: jnp.broadcast_to(lse, (DIL_BLOCK, HEAD_DIM))

    for a in range(DIL_DEINT):
        idx = pl.ds(a, n_sub, stride=DIL_DEINT)
        q4[a] = q_ref[idx, :]
        k4[a] = k_ref[idx, :]
        v4[a] = v_ref[idx, :]

    @pl.when(first)
    def _():
        k4p[...] = jnp.zeros(k4p.shape, F32)
        v4p[...] = jnp.zeros(v4p.shape, F32)

    def biases(br):
        bias = bias_ref[br]
        bias_first = jnp.concatenate(
            [jnp.where(first, neg_inf, bias[:, :DIL_BLOCK]), bias[:, DIL_BLOCK:]], axis=1)
        return bias, bias_first

    bias, bias_first = biases(0)
    k_prev, v_prev = bf(kp_ref[...]), bf(vp_ref[...])
    for m in range(DIL_SUPER // DIL_BLOCK):
        idx = pl.ds(m * DIL_BLOCK, DIL_BLOCK)
        k_cur, v_cur = bf(k_ref[idx, :]), bf(v_ref[idx, :])
        o, lse = _dil_block(bf(q_ref[idx, :]), k_prev, k_cur, v_prev, v_cur, bias_first if m == 0 else bias)
        o1[idx, :] = o
        l1[idx, :] = rep(lse)
        k_prev, v_prev = k_cur, v_cur

    bias, bias_first = biases(1)
    for a in range(DIL_DEINT):
        last = pl.ds(n_sub - DIL_BLOCK, DIL_BLOCK)
        k_prev, v_prev = bf(k4p[a, last, :]), bf(v4p[a, last, :])
        for m in range(n_sub // DIL_BLOCK):
            idx = pl.ds(m * DIL_BLOCK, DIL_BLOCK)
            k_cur, v_cur = bf(k4[a, idx, :]), bf(v4[a, idx, :])
            o, lse = _dil_block(bf(q4[a, idx, :]), k_prev, k_cur, v_prev, v_cur, bias_first if m == 0 else bias)
            o4[a, idx, :] = o
            l4[a, idx, :] = rep(lse)
            k_prev, v_prev = k_cur, v_cur

    _, bias_first = biases(2)
    for a in range(DIL_DEINT):
        for r in range(DIL_DEINT):
            idx = pl.ds(r, DIL_BLOCK, stride=DIL_DEINT)
            o, lse = _dil_block(bf(q4.at[a][idx, :]), bf(k4p.at[a][idx, :]), bf(k4.at[a][idx, :]),
                                bf(v4p.at[a][idx, :]), bf(v4.at[a][idx, :]), bias_first)
            o16.at[a][idx, :] = o
            l16.at[a][idx, :] = rep(lse)

    k4p[...] = k4[...]
    v4p[...] = v4[...]

    for a in range(DIL_DEINT):
        def merge(c, carry, a=a):
            idx = pl.ds(pl.multiple_of(c * DIL_BLOCK, DIL_BLOCK), DIL_BLOCK)
            nat = pl.ds(c * (DIL_BLOCK * DIL_DEINT) + a, DIL_BLOCK, stride=DIL_DEINT)
            la, lb, lc = l1[nat, :], l4[a, idx, :], l16[a, idx, :]
            mx = jnp.maximum(jnp.maximum(la, lb), lc)
            ea, eb, ec = jnp.exp2(la - mx), jnp.exp2(lb - mx), jnp.exp2(lc - mx)
            num = ea * o1[nat, :] + eb * o4[a, idx, :] + ec * o16[a, idx, :]
            out_f[nat, :] = num / (ea + eb + ec)
            return carry

        lax.fori_loop(0, n_sub // DIL_BLOCK, merge, 0)

    def emit(t, carry):
        idx = pl.ds(pl.multiple_of(t * DIL_BLOCK, DIL_BLOCK), DIL_BLOCK)
        o_ref[idx, :] = out_f[idx, :].astype(o_ref.dtype)
        return carry

    lax.fori_loop(0, DIL_SUPER // DIL_BLOCK, emit, 0)


def _dilated_attention(qkv, bias_tiles):
    assert tuple(d for _, d in DIL_PATTERNS) == (1, DIL_DEINT, DIL_DEINT ** 2)
    s = qkv.shape[0]
    blk = (DIL_SUPER, HEAD_DIM)
    tail = (DIL_BLOCK, HEAD_DIM)
    n_tail = DIL_SUPER // DIL_BLOCK
    cur = lambda off: (lambda h, b: (b, off + h))
    prev_tail = lambda off: (lambda h, b: (jnp.maximum(b * n_tail - 1, 0), off + h))
    cls = (DIL_DEINT, DIL_SUPER // DIL_DEINT, HEAD_DIM)
    scratch = ([pltpu.VMEM(cls, F32)] * 5
               + [pltpu.VMEM(blk, F32)] * 2 + [pltpu.VMEM(cls, F32)] * 4
               + [pltpu.VMEM(blk, F32)])
    return pl.pallas_call(
        _dil_kernel,
        grid=(N_HEADS, s // DIL_SUPER),
        in_specs=[
            pl.BlockSpec(blk, cur(0)),
            pl.BlockSpec(blk, cur(N_HEADS)),
            pl.BlockSpec(blk, cur(2 * N_HEADS)),
            pl.BlockSpec(tail, prev_tail(N_HEADS)),
            pl.BlockSpec(tail, prev_tail(2 * N_HEADS)),
            pl.BlockSpec((None, len(DIL_PATTERNS), DIL_BLOCK, 2 * DIL_BLOCK), lambda h, b: (h, 0, 0, 0)),
        ],
        out_specs=pl.BlockSpec(blk, lambda h, b: (b, h)),
        out_shape=jax.ShapeDtypeStruct((s, ATT_WIDTH), BF16),
        scratch_shapes=scratch,
        compiler_params=_params(2),
        name="dilated_attention",
    )(qkv, qkv, qkv, qkv, qkv, bias_tiles)


def _out_proj_kernel(a_ref, b_ref, ga_ref, gb_ref, w_ref, x_ref, gf_ref, o_ref, xg_ref, rs_ref, mix_scr,
                     *, d_model):
    j = pl.program_id(1)

    @pl.when(j == 0)
    def _():
        a = a_ref[...].astype(F32)
        b = b_ref[...].astype(F32)
        mix_scr[:, :ATT_WIDTH] = (a * _rms_scale(a) * ga_ref[...]).astype(BF16)
        mix_scr[:, ATT_WIDTH:] = (b * _rms_scale(b) * gb_ref[...]).astype(BF16)
        rs_ref[...] = jnp.zeros(rs_ref.shape, F32)

    x1 = x_ref[...] + _dot(mix_scr[...], w_ref[...])
    o_ref[...] = x1
    xg_ref[...] = _pack_rows((x1 * gf_ref[...]).astype(BF16))
    rs_ref[...] += jnp.sum(x1 * x1, axis=1, keepdims=True)

    @pl.when(j == pl.num_programs(1) - 1)
    def _():
        rs_ref[...] = lax.rsqrt(rs_ref[...] * (1.0 / d_model) + RMS_EPS)


def _out_proj(a, b, ga, gb, w, x, g_ffn, tm, tn):
    s, d = x.shape
    row = lambda i, j: (i, 0)
    tile = lambda i, j: (i, j)
    return pl.pallas_call(
        functools.partial(_out_proj_kernel, d_model=d),
        grid=(s // tm, d // tn),
        in_specs=[
            pl.BlockSpec((tm, ATT_WIDTH), row),
            pl.BlockSpec((tm, ATT_WIDTH), row),
            pl.BlockSpec((1, ATT_WIDTH), lambda i, j: (0, 0)),
            pl.BlockSpec((1, ATT_WIDTH), lambda i, j: (0, 0)),
            pl.BlockSpec((2 * ATT_WIDTH, tn), lambda i, j: (0, j)),
            pl.BlockSpec((tm, tn), tile),
            pl.BlockSpec((1, tn), lambda i, j: (0, j)),
        ],
        out_specs=[pl.BlockSpec((tm, tn), tile), pl.BlockSpec((tm // 2, tn), tile),
                   pl.BlockSpec((tm, LANES), row)],
        out_shape=[jax.ShapeDtypeStruct((s, d), F32), jax.ShapeDtypeStruct((s // 2, d), U32),
                   jax.ShapeDtypeStruct((s, LANES), F32)],
        scratch_shapes=[pltpu.VMEM((tm, 2 * ATT_WIDTH), BF16)],
        compiler_params=_params(2),
        name="out_proj",
    )(a, b, ga, gb, w, x, g_ffn)


def _ffn_tile_start(j, tn, n_features):
    return pl.multiple_of(jnp.minimum(j * tn, n_features - tn), LANES)


def _ffn_up_kernel(xg_ref, rs_ref, w_hbm, cw_ref, cb_ref, o_ref, wg_stage, wv_stage, wg_scr, wv_scr, tail_scr,
                   sem, *, rows_per_chunk, n_features, last_shift):
    tm, tn = rs_ref.shape[0], o_ref.shape[1]

    def copies(jj):
        c0 = _ffn_tile_start(jj, tn, n_features)
        return [pltpu.make_async_copy(w_hbm.at[:, pl.ds(c0, tn)], wg_stage, sem.at[0]),
                pltpu.make_async_copy(w_hbm.at[:, pl.ds(n_features + c0, tn)], wv_stage, sem.at[1])]

    def cast():
        wg_scr[...] = wg_stage[...].astype(BF16)
        wv_scr[...] = wv_stage[...].astype(BF16)

    _stage_weight_tile(pl.program_id(0), pl.num_programs(0), copies, cast)

    @pl.when(pl.program_id(1) == 0)
    def _():
        tail_scr[...] = jnp.zeros(tail_scr.shape, F32)

    cw = cw_ref[...]
    cb = cb_ref[...]
    tail = tail_scr[...]
    sub = lax.broadcasted_iota(jnp.int32, tail.shape, 0)
    for r0 in range(0, tm, rows_per_chunk):
        rows = slice(r0, r0 + rows_per_chunk)
        packed_rows = slice(r0 // 2, (r0 + rows_per_chunk) // 2)
        xg = _unpack_rows(xg_ref[packed_rows, :])
        rs = jnp.concatenate([rs_ref[rows, :]] * (tn // LANES), axis=1)
        gate = _dot(xg, wg_scr[...]) * rs
        val = _dot(xg, wv_scr[...]) * rs

        def shifted(k, gate=gate, tail=tail):
            body = pltpu.roll(gate, k, axis=0)
            head = jnp.where(sub < k, pltpu.roll(tail, k, axis=0), body[:SUBLANES, :])
            return jnp.concatenate([head, body[SUBLANES:, :]], axis=0)

        conv = shifted(2) * cw[0:1, :] + shifted(1) * cw[1:2, :] + gate * cw[2:3, :] + cb
        act = conv * (1.0 / (1.0 + jnp.exp(-conv))) * val
        if last_shift:
            moved = jnp.concatenate([act[:, last_shift:], act[:, :last_shift]], axis=1)
            act = jnp.where(pl.program_id(0) == pl.num_programs(0) - 1, moved, act)
        o_ref[packed_rows, :] = _pack_rows(act.astype(BF16))
        tail = gate[rows_per_chunk - SUBLANES:, :]
    tail_scr[...] = tail


def _ffn_up(xg, rs, w_up, conv_w, conv_b, tm, tn):
    s, d = rs.shape[0], xg.shape[1]
    f = conv_w.shape[1]
    assert f % LANES == 0 and f >= tn
    n_f = -(-f // tn)
    start = lambda j: _ffn_tile_start(j, tn, f)
    return pl.pallas_call(
        functools.partial(_ffn_up_kernel, rows_per_chunk=min(128, tm), n_features=f, last_shift=n_f * tn - f),
        grid=(n_f, s // tm),
        in_specs=[
            pl.BlockSpec((tm // 2, d), lambda j, i: (i, 0)),
            pl.BlockSpec((tm, LANES), lambda j, i: (i, 0)),
            pl.BlockSpec(memory_space=pl.ANY),
            pl.BlockSpec((pl.Element(CONV_WIDTH), pl.Element(tn)), lambda j, i: (0, start(j))),
            pl.BlockSpec((pl.Element(1), pl.Element(tn)), lambda j, i: (0, start(j))),
        ],
        out_specs=pl.BlockSpec((tm // 2, tn), lambda j, i: (i, j)),
        out_shape=jax.ShapeDtypeStruct((s // 2, n_f * tn), U32),
        scratch_shapes=[pltpu.VMEM((d, tn), F32), pltpu.VMEM((d, tn), F32),
                        pltpu.VMEM((d, tn), BF16), pltpu.VMEM((d, tn), BF16),
                        pltpu.VMEM((SUBLANES, tn), F32), pltpu.SemaphoreType.DMA((2,))],
        compiler_params=_params(2),
        name="ffn_up",
    )(xg, rs, w_up, conv_w, conv_b)


def _ffn_down_kernel(a_ref, w_ref, x_ref, g_ref, o_ref, *, tn):
    n = pl.program_id(1)
    cols = pl.ds(pl.multiple_of(n * tn, tn), tn)
    o_ref[:, cols] = x_ref[...] + _dot(_unpack_rows(a_ref[...]), w_ref[...])

    @pl.when(n == pl.num_programs(1) - 1)
    def _():
        y = o_ref[...]
        o_ref[...] = y * _rms_scale(y) * g_ref[...]


def _ffn_down(act, w, x, g, tm, tn):
    s, d = x.shape
    f = w.shape[0]
    return pl.pallas_call(
        functools.partial(_ffn_down_kernel, tn=tn),
        grid=(s // tm, d // tn),
        in_specs=[
            pl.BlockSpec((tm // 2, f), lambda i, n: (i, 0)),
            pl.BlockSpec((f, tn), lambda i, n: (0, n)),
            pl.BlockSpec((tm, tn), lambda i, n: (i, n)),
            pl.BlockSpec((1, d), lambda i, n: (0, 0)),
        ],
        out_specs=pl.BlockSpec((tm, d), lambda i, n: (i, 0)),
        out_shape=jax.ShapeDtypeStruct((s, d), F32),
        compiler_params=_params(2, vmem_limit=60 * 1024 * 1024),
        name="ffn_down",
    )(act, w, x, g)


def _layer(x, attn_g, w_in, forget_b, bias_tiles, fox_g, dil_g, w_out, ffn_g, w_up, conv_w, conv_b,
           w_down, final_g):
    s, d = x.shape
    c0 = 3 * ATT_WIDTH
    c1 = c0 + N_HEADS
    tm_small = min(512, s)
    tm_big = min(1024, s)

    w_in_t = w_in.T
    w_f = jnp.pad(w_in_t[c0:c1], ((0, LANES - N_HEADS), (0, 0))).astype(BF16)
    f_b = jnp.pad(forget_b.astype(F32), (0, LANES - N_HEADS))[None, :]
    h, kc = _norm_gate(x, attn_g[None, :], w_f, f_b, tm_small)

    qkv_a = _in_proj(h, w_in_t, 0, BF16, tm_big, 1024, "in_proj_fox")
    qkv_b = _in_proj(h, w_in_t, c1, F32, tm_big, 1024, "in_proj_dil")

    out_a, w_out_bf, w_down_bf = _fox_attention(qkv_a, kc, (w_out, w_down), min(2048, s), tm_big, 256)
    out_b = _dilated_attention(qkv_b, bias_tiles)

    x1, xg, rs = _out_proj(out_a, out_b, fox_g[None, :], dil_g[None, :], w_out_bf, x,
                           ffn_g[None, :], tm_big, min(512, d))
    act = _ffn_up(xg, rs, w_up, conv_w, conv_b[None, :], tm_big, min(512, conv_w.shape[1]))
    return _ffn_down(act, w_down_bf, x1, final_g[None, :], tm_small, min(256, d))


def kernel(x, attn_norm_g, w_in, fox_forget_b, rel_bias_table, fox_out_norm_g, dil_out_norm_g, w_out,
           ffn_norm_g, w_up, conv_w, conv_b, w_down, final_norm_g):
    depth = w_in.shape[0]
    assert depth == 1 and x.shape[0] == 1, "single layer, single sequence"
    assert x.shape[1] % DIL_SUPER == 0 and conv_w.shape[-1] % 256 == 0
    bias_tiles = _dil_bias_tiles(rel_bias_table)
    y = _layer(x[0], attn_norm_g[0], w_in[0], fox_forget_b[0], bias_tiles, fox_out_norm_g[0],
               dil_out_norm_g[0], w_out[0], ffn_norm_g[0], w_up[0], conv_w[0], conv_b[0], w_down[0],
               final_norm_g)
    return y[None]
```

```python
import functools
import math

import jax
import jax.numpy as jnp
import numpy as np
from jax import lax
from jax.experimental import pallas as pl
from jax.experimental.pallas import tpu as pltpu

HEAD_DIM = 128
N_HEADS = 16
ATT_WIDTH = N_HEADS * HEAD_DIM
DIL_PATTERNS = ((128, 1), (512, 4), (2048, 16))
DIL_BLOCK = 128
DIL_SUPER = 2048
DIL_DEINT = 4
FOX_HEADS_PER_STEP = 1
N_BUCKETS = 32
MAX_DISTANCE = 2048
RMS_EPS = 1e-6
CONV_WIDTH = 3
LOG2E = math.log2(math.e)
N_BIAS_COLS = 3
LANES = 128
SUBLANES = 8
VMEM_LIMIT = 56 * 1024 * 1024

F32 = jnp.float32
BF16 = jnp.bfloat16
U32 = jnp.uint32


def _dot(a, b):
    return jnp.dot(a, b, preferred_element_type=F32)


def _dot_nt(a, b):
    return lax.dot_general(a, b, (((1,), (1,)), ((), ())), preferred_element_type=F32)


def _pack_rows(v):
    return pltpu.bitcast(v, U32)


def _unpack_rows(v):
    return pltpu.bitcast(v, BF16)


def _rms_scale(x):
    return lax.rsqrt(jnp.mean(x * x, axis=-1, keepdims=True) + RMS_EPS)


def _cumsum_rows(v):
    n = v.shape[0]
    rows = lax.broadcasted_iota(jnp.int32, v.shape, 0)
    s = 1
    while s < n:
        v = v + jnp.where(rows >= s, pltpu.roll(v, s, axis=0), 0.0)
        s *= 2
    return v


def _params(n_grid_dims, vmem_limit=VMEM_LIMIT):
    return pltpu.CompilerParams(dimension_semantics=("arbitrary",) * n_grid_dims,
                                vmem_limit_bytes=vmem_limit)


def _truncate_to_bf16(v):
    bits = lax.bitcast_convert_type(v, U32) & jnp.uint32(0xFFFF0000)
    return lax.bitcast_convert_type(bits, F32)


def _norm_gate_kernel(x_ref, g_ref, wf_ref, fb_ref, h_ref, kc_ref, carry_scr):
    x = x_ref[...]
    hb = (x * _rms_scale(x) * g_ref[...]).astype(BF16)
    h_ref[...] = _pack_rows(hb)
    z = _dot_nt(hb, wf_ref[...]) + fb_ref[...]
    log_f = jnp.minimum(z, 0.0) - jnp.log1p(jnp.exp(-jnp.abs(z)))

    @pl.when(pl.program_id(0) == 0)
    def _():
        carry_scr[...] = jnp.zeros(carry_scr.shape, F32)

    c = _cumsum_rows(log_f) + carry_scr[0:1, :]
    carry_scr[...] = jnp.broadcast_to(c[c.shape[0] - 1:, :], carry_scr.shape)

    ck = c * (-LOG2E)
    hi = _truncate_to_bf16(ck)
    mid = _truncate_to_bf16(ck - hi)
    lo = ck - hi - mid
    lane = lax.broadcasted_iota(jnp.int32, c.shape, 1)
    for head in range(N_HEADS):
        col = slice(head, head + 1)
        cols = jnp.where(lane == 0, hi[:, col], jnp.where(lane == 1, mid[:, col], jnp.where(lane == 2, lo[:, col], 0.0)))
        kc_ref[head] = cols.astype(BF16)


def _norm_gate(x, g, w_f, f_b, tm):
    s, d = x.shape
    return pl.pallas_call(
        _norm_gate_kernel,
        grid=(s // tm,),
        in_specs=[
            pl.BlockSpec((tm, d), lambda i: (i, 0)),
            pl.BlockSpec((1, d), lambda i: (0, 0)),
            pl.BlockSpec((LANES, d), lambda i: (0, 0)),
            pl.BlockSpec((1, LANES), lambda i: (0, 0)),
        ],
        out_specs=[pl.BlockSpec((tm // 2, d), lambda i: (i, 0)),
                   pl.BlockSpec((N_HEADS, tm, HEAD_DIM), lambda i: (0, i, 0))],
        out_shape=[jax.ShapeDtypeStruct((s // 2, d), U32), jax.ShapeDtypeStruct((N_HEADS, s, HEAD_DIM), BF16)],
        scratch_shapes=[pltpu.VMEM((SUBLANES, LANES), F32)],
        compiler_params=_params(1),
        name="norm_gate",
    )(x, g, w_f, f_b)


def _stage_weight_tile(j, n_j, copies, cast):
    @pl.when(pl.program_id(1) == 0)
    def _():
        @pl.when(j == 0)
        def _():
            for cp in copies(j):
                cp.start()

        for cp in copies(j):
            cp.wait()
        cast()

        @pl.when(j + 1 < n_j)
        def _():
            for cp in copies(j + 1):
                cp.start()


def _in_proj_kernel(h_ref, wt_hbm, o_ref, w_stage, w_scr, sem, *, row0):
    j = pl.program_id(0)
    tn = w_scr.shape[0]

    def copies(jj):
        rows = pl.ds(pl.multiple_of(row0 + jj * tn, SUBLANES), tn)
        return [pltpu.make_async_copy(wt_hbm.at[rows, :], w_stage, sem.at[0])]

    def cast():
        w_scr[...] = w_stage[...].astype(BF16)

    _stage_weight_tile(j, pl.num_programs(0), copies, cast)
    scale = jnp.where(j * tn < ATT_WIDTH, HEAD_DIM ** -0.5 * LOG2E, 1.0).astype(F32)
    o_ref[...] = (_dot_nt(_unpack_rows(h_ref[...]), w_scr[...]) * scale).astype(o_ref.dtype)


def _in_proj(h, w_in_t, row0, out_dtype, tm, tn, name):
    s, d = 2 * h.shape[0], h.shape[1]
    assert ATT_WIDTH % tn == 0
    kern = functools.partial(_in_proj_kernel, row0=row0)
    return pl.pallas_call(
        kern,
        grid=(3 * ATT_WIDTH // tn, s // tm),
        in_specs=[
            pl.BlockSpec((tm // 2, d), lambda j, i: (i, 0)),
            pl.BlockSpec(memory_space=pl.ANY),
        ],
        out_specs=pl.BlockSpec((tm, tn), lambda j, i: (i, j)),
        out_shape=jax.ShapeDtypeStruct((s, 3 * ATT_WIDTH), out_dtype),
        scratch_shapes=[pltpu.VMEM((tn, d), F32), pltpu.VMEM((tn, d), BF16), pltpu.SemaphoreType.DMA((1,))],
        compiler_params=_params(2),
        name=name,
    )(h, w_in_t)


def _side_cast_rows(n_rows, n_steps):
    rows = -(-n_rows // n_steps)
    rows += -rows % (2 * SUBLANES)
    while n_rows % rows:
        rows += 2 * SUBLANES
    return rows


def _fox_kernel(q_ref, k_ref, kc_ref, v_ref, *rest, tq, tk, ts, hp, side_blocks):
    n_side = len(side_blocks)
    side_src, (o_ref, *side_dst), (acc_scr, m_scr) = rest[:n_side], rest[n_side:2 * n_side + 1], rest[2 * n_side + 1:]
    step_id = pl.program_id(0) * pl.num_programs(1) + pl.program_id(1)
    for src, dst, n_blocks in zip(side_src, side_dst, side_blocks):
        @pl.when(step_id < n_blocks)
        def _(src=src, dst=dst):
            dst[...] = src[...].astype(dst.dtype)

    i = pl.program_id(1)
    lane = lax.broadcasted_iota(jnp.int32, (ts, HEAD_DIM), 1)
    bias_ones = (lane < N_BIAS_COLS).astype(BF16)
    ones = jnp.ones((tk, HEAD_DIM), BF16)
    m_scr[...] = jnp.full(m_scr.shape, -jnp.inf, F32)
    acc_scr[...] = jnp.zeros(acc_scr.shape, F32)

    def step(kb, diag):
        rows_k = pl.ds(pl.multiple_of(kb * tk, tk), tk)
        for hh, r0 in ((hh, r0) for r0 in range(0, tq, ts) for hh in range(hp)):
            hd = slice(hh * HEAD_DIM, (hh + 1) * HEAD_DIM)
            k_aug = jnp.concatenate([k_ref[rows_k, hd], kc_ref[hh, rows_k, :]], axis=1)
            v_aug = jnp.concatenate([v_ref[rows_k, hd], ones], axis=1)
            if diag is not None and r0 + ts - 1 < diag * tk:
                continue
            rs = slice(r0, r0 + ts)
            nk = tk if diag is None else min(tk, r0 + ts - diag * tk)
            q_aug = jnp.concatenate([q_ref[rs, hd], bias_ones], axis=1)
            s = _dot_nt(q_aug, k_aug[:nk])
            if diag is not None and r0 < diag * tk + nk - 1:
                rows = lax.broadcasted_iota(jnp.int32, s.shape, 0)
                cols = lax.broadcasted_iota(jnp.int32, s.shape, 1)
                s = jnp.where(cols + (diag * tk - r0) <= rows, s, -jnp.inf)
            chunks = [s[:, c * LANES:(c + 1) * LANES] for c in range(nk // LANES)]
            m_cur = functools.reduce(jnp.maximum, chunks)
            m_prev = m_scr[hh, rs, :]
            m_new = jnp.maximum(m_prev, jnp.max(m_cur, axis=1, keepdims=True))
            alpha = jnp.exp2(m_prev - m_new)
            p = jnp.concatenate([jnp.exp2(ch - m_new) for ch in chunks], axis=1).astype(BF16)
            acc_scr[hh, rs, :] = (acc_scr[hh, rs, :] * jnp.concatenate([alpha, alpha], axis=1)
                                  + _dot(p, v_aug[:nk]))
            m_scr[hh, rs, :] = m_new

    def pair(t, carry):
        step(2 * t, None)
        step(2 * t + 1, None)
        return carry

    n_diag = tq // tk
    n_pairs = i * (n_diag // 2)

    def diagonal():
        for dg in range(n_diag):
            step(i * n_diag + dg, dg)

    @pl.when(i == 0)
    def _():
        diagonal()

    @pl.when(i > 0)
    def _():
        lax.fori_loop(0, n_pairs - 1, pair, 0)
        pair(n_pairs - 1, 0)
        diagonal()

    for hh in range(hp):
        acc = acc_scr[hh]
        o_ref[:, hh * HEAD_DIM:(hh + 1) * HEAD_DIM] = (acc[:, :HEAD_DIM] / acc[:, HEAD_DIM:]).astype(o_ref.dtype)


def _fox_attention(qkv, kc, side_weights, tq, tk, ts, hp):
    s = qkv.shape[0]
    assert (tq // tk) % 2 == 0 and tq % ts == 0
    n_i = s // tq
    n_groups = N_HEADS // hp
    n_steps = n_groups * n_i
    side_specs, side_blocks = [], []
    for w in side_weights:
        rows = _side_cast_rows(w.shape[0], n_steps)
        n_blocks = w.shape[0] // rows
        side_blocks.append(n_blocks)
        side_specs.append(pl.BlockSpec(
            (rows, w.shape[1]), lambda h, i, n_blocks=n_blocks: (jnp.minimum(h * n_i + i, n_blocks - 1), 0)))
    kern = functools.partial(_fox_kernel, tq=tq, tk=tk, ts=ts, hp=hp, side_blocks=tuple(side_blocks))
    return pl.pallas_call(
        kern,
        grid=(n_groups, n_i),
        in_specs=[
            pl.BlockSpec((tq, hp * HEAD_DIM), lambda h, i: (i, h)),
            pl.BlockSpec((s, hp * HEAD_DIM), lambda h, i: (0, n_groups + h)),
            pl.BlockSpec((hp, s, HEAD_DIM), lambda h, i: (h, 0, 0)),
            pl.BlockSpec((s, hp * HEAD_DIM), lambda h, i: (0, 2 * n_groups + h)),
        ] + side_specs,
        out_specs=[pl.BlockSpec((tq, hp * HEAD_DIM), lambda h, i: (i, h))] + side_specs,
        out_shape=[jax.ShapeDtypeStruct((s, ATT_WIDTH), BF16)]
        + [jax.ShapeDtypeStruct(w.shape, BF16) for w in side_weights],
        scratch_shapes=[pltpu.VMEM((hp, tq, 2 * HEAD_DIM), F32), pltpu.VMEM((hp, tq, HEAD_DIM), F32)],
        compiler_params=_params(2),
        name="fox_attention",
    )(qkv, qkv, kc, qkv, *side_weights)


def _t5_bucket_np(dist):
    max_exact = N_BUCKETS // 2
    d_f = np.maximum(dist, 1).astype(np.float32)
    ratio = np.log(d_f / np.float32(max_exact)) / np.float32(math.log(MAX_DISTANCE / max_exact))
    large = max_exact + (ratio * np.float32(N_BUCKETS - max_exact)).astype(np.int32)
    large = np.minimum(large, N_BUCKETS - 1)
    return np.where(dist < max_exact, dist, large).astype(np.int32)


def _dil_bias_rows(rel_bias_table):
    n_valid = DIL_BLOCK + 1
    dist = np.stack([(DIL_BLOCK - np.arange(n_valid)) * dil for _, dil in DIL_PATTERNS])
    row = rel_bias_table.astype(F32)[_t5_bucket_np(dist)]
    row = jnp.transpose(row, (2, 0, 1)) * LOG2E
    pad = jnp.full(row.shape[:2] + (2 * DIL_BLOCK - n_valid,), -jnp.inf, F32)
    return jnp.concatenate([row, pad], axis=-1)[:, :, None, :]


def _dil_block(q, k_prev, k_cur, v_prev, v_cur, bias):
    k2 = jnp.concatenate([k_prev, k_cur], axis=0)
    v2 = jnp.concatenate([v_prev, v_cur], axis=0)
    s = _dot_nt(q, k2) + bias
    m = jnp.max(s, axis=1, keepdims=True)
    p = jnp.exp2(s - m)
    den = jnp.sum(p, axis=1, keepdims=True)
    o = _dot(p.astype(BF16), v2) * (1.0 / den)
    return o, m + jnp.log(den) * LOG2E


def _dil_kernel(q_ref, k_ref, v_ref, kp_ref, vp_ref, bias_ref, o_ref,
                q4, k4, v4, k4p, v4p, o1, l1, o4, l4, o16, l16, out_f, bias_scr):
    first = pl.program_id(1) == 0
    neg_inf = jnp.full((DIL_BLOCK, DIL_BLOCK), -jnp.inf, F32)
    n_sub = DIL_SUPER // DIL_DEINT
    bf = lambda t: t.astype(BF16)
    rep = lambda lse: jnp.broadcast_to(lse, (DIL_BLOCK, HEAD_DIM))

    for a in range(DIL_DEINT):
        idx = pl.ds(a, n_sub, stride=DIL_DEINT)
        q4[a] = q_ref[idx, :]
        k4[a] = k_ref[idx, :]
        v4[a] = v_ref[idx, :]

    @pl.when(first)
    def _():
        k4p[...] = jnp.zeros(k4p.shape, F32)
        v4p[...] = jnp.zeros(v4p.shape, F32)
        for br in range(len(DIL_PATTERNS)):
            row = jnp.broadcast_to(bias_ref[br], (DIL_BLOCK, 2 * DIL_BLOCK))
            bias_scr[br] = pltpu.roll(row, 0, 1, stride=1, stride_axis=0)

    def biases(br):
        bias = bias_scr[br]
        bias_first = jnp.concatenate(
            [jnp.where(first, neg_inf, bias[:, :DIL_BLOCK]), bias[:, DIL_BLOCK:]], axis=1)
        return bias, bias_first

    bias, bias_first = biases(0)
    k_prev, v_prev = bf(kp_ref[...]), bf(vp_ref[...])
    for m in range(DIL_SUPER // DIL_BLOCK):
        idx = pl.ds(m * DIL_BLOCK, DIL_BLOCK)
        k_cur, v_cur = bf(k_ref[idx, :]), bf(v_ref[idx, :])
        o, lse = _dil_block(bf(q_ref[idx, :]), k_prev, k_cur, v_prev, v_cur, bias_first if m == 0 else bias)
        o1[idx, :] = o
        l1[idx, :] = rep(lse)
        k_prev, v_prev = k_cur, v_cur

    bias, bias_first = biases(1)
    for a in range(DIL_DEINT):
        last = pl.ds(n_sub - DIL_BLOCK, DIL_BLOCK)
        k_prev, v_prev = bf(k4p[a, last, :]), bf(v4p[a, last, :])
        for m in range(n_sub // DIL_BLOCK):
            idx = pl.ds(m * DIL_BLOCK, DIL_BLOCK)
            k_cur, v_cur = bf(k4[a, idx, :]), bf(v4[a, idx, :])
            o, lse = _dil_block(bf(q4[a, idx, :]), k_prev, k_cur, v_prev, v_cur, bias_first if m == 0 else bias)
            o4[a, idx, :] = o
            l4[a, idx, :] = rep(lse)
            k_prev, v_prev = k_cur, v_cur

    _, bias_first = biases(2)
    for a in range(DIL_DEINT):
        for r in range(DIL_DEINT):
            idx = pl.ds(r, DIL_BLOCK, stride=DIL_DEINT)
            o, lse = _dil_block(bf(q4.at[a][idx, :]), bf(k4p.at[a][idx, :]), bf(k4.at[a][idx, :]),
                                bf(v4p.at[a][idx, :]), bf(v4.at[a][idx, :]), bias_first)
            o16.at[a][idx, :] = o
            l16.at[a][idx, :] = rep(lse)

    k4p[...] = k4[...]
    v4p[...] = v4[...]

    for a in range(DIL_DEINT):
        def merge(c, carry, a=a):
            idx = pl.ds(pl.multiple_of(c * DIL_BLOCK, DIL_BLOCK), DIL_BLOCK)
            nat = pl.ds(c * (DIL_BLOCK * DIL_DEINT) + a, DIL_BLOCK, stride=DIL_DEINT)
            la, lb, lc = l1[nat, :], l4[a, idx, :], l16[a, idx, :]
            mx = jnp.maximum(jnp.maximum(la, lb), lc)
            ea, eb, ec = jnp.exp2(la - mx), jnp.exp2(lb - mx), jnp.exp2(lc - mx)
            num = ea * o1[nat, :] + eb * o4[a, idx, :] + ec * o16[a, idx, :]
            out_f[nat, :] = num / (ea + eb + ec)
            return carry

        lax.fori_loop(0, n_sub // DIL_BLOCK, merge, 0)

    def emit(t, carry):
        idx = pl.ds(pl.multiple_of(t * DIL_BLOCK, DIL_BLOCK), DIL_BLOCK)
        o_ref[idx, :] = out_f[idx, :].astype(o_ref.dtype)
        return carry

    lax.fori_loop(0, DIL_SUPER // DIL_BLOCK, emit, 0)


def _dilated_attention(qkv, bias_rows):
    assert tuple(d for _, d in DIL_PATTERNS) == (1, DIL_DEINT, DIL_DEINT ** 2)
    s = qkv.shape[0]
    blk = (DIL_SUPER, HEAD_DIM)
    tail = (DIL_BLOCK, HEAD_DIM)
    n_tail = DIL_SUPER // DIL_BLOCK
    cur = lambda off: (lambda h, b: (b, off + h))
    prev_tail = lambda off: (lambda h, b: (jnp.maximum(b * n_tail - 1, 0), off + h))
    cls = (DIL_DEINT, DIL_SUPER // DIL_DEINT, HEAD_DIM)
    scratch = ([pltpu.VMEM(cls, F32)] * 5
               + [pltpu.VMEM(blk, F32)] * 2 + [pltpu.VMEM(cls, F32)] * 4
               + [pltpu.VMEM(blk, F32)]
               + [pltpu.VMEM((len(DIL_PATTERNS), DIL_BLOCK, 2 * DIL_BLOCK), F32)])
    return pl.pallas_call(
        _dil_kernel,
        grid=(N_HEADS, s // DIL_SUPER),
        in_specs=[
            pl.BlockSpec(blk, cur(0)),
            pl.BlockSpec(blk, cur(N_HEADS)),
            pl.BlockSpec(blk, cur(2 * N_HEADS)),
            pl.BlockSpec(tail, prev_tail(N_HEADS)),
            pl.BlockSpec(tail, prev_tail(2 * N_HEADS)),
            pl.BlockSpec((None, len(DIL_PATTERNS), 1, 2 * DIL_BLOCK), lambda h, b: (h, 0, 0, 0)),
        ],
        out_specs=pl.BlockSpec(blk, lambda h, b: (b, h)),
        out_shape=jax.ShapeDtypeStruct((s, ATT_WIDTH), BF16),
        scratch_shapes=scratch,
        compiler_params=_params(2),
        name="dilated_attention",
    )(qkv, qkv, qkv, qkv, qkv, bias_rows)


def _out_proj_kernel(a_ref, b_ref, ga_ref, gb_ref, w_ref, x_ref, gf_ref, o_ref, xg_ref, rs_ref, mix_scr,
                     *, d_model):
    j = pl.program_id(1)

    @pl.when(j == 0)
    def _():
        a = a_ref[...].astype(F32)
        b = b_ref[...].astype(F32)
        mix_scr[:, :ATT_WIDTH] = (a * _rms_scale(a) * ga_ref[...]).astype(BF16)
        mix_scr[:, ATT_WIDTH:] = (b * _rms_scale(b) * gb_ref[...]).astype(BF16)
        rs_ref[...] = jnp.zeros(rs_ref.shape, F32)

    x1 = x_ref[...] + _dot(mix_scr[...], w_ref[...])
    o_ref[...] = x1
    xg_ref[...] = _pack_rows((x1 * gf_ref[...]).astype(BF16))
    rs_ref[...] += jnp.sum(x1 * x1, axis=1, keepdims=True)

    @pl.when(j == pl.num_programs(1) - 1)
    def _():
        rs_ref[...] = lax.rsqrt(rs_ref[...] * (1.0 / d_model) + RMS_EPS)


def _out_proj(a, b, ga, gb, w, x, g_ffn, tm, tn):
    s, d = x.shape
    row = lambda i, j: (i, 0)
    tile = lambda i, j: (i, j)
    return pl.pallas_call(
        functools.partial(_out_proj_kernel, d_model=d),
        grid=(s // tm, d // tn),
        in_specs=[
            pl.BlockSpec((tm, ATT_WIDTH), row),
            pl.BlockSpec((tm, ATT_WIDTH), row),
            pl.BlockSpec((1, ATT_WIDTH), lambda i, j: (0, 0)),
            pl.BlockSpec((1, ATT_WIDTH), lambda i, j: (0, 0)),
            pl.BlockSpec((2 * ATT_WIDTH, tn), lambda i, j: (0, j)),
            pl.BlockSpec((tm, tn), tile),
            pl.BlockSpec((1, tn), lambda i, j: (0, j)),
        ],
        out_specs=[pl.BlockSpec((tm, tn), tile), pl.BlockSpec((tm // 2, tn), tile),
                   pl.BlockSpec((tm, LANES), row)],
        out_shape=[jax.ShapeDtypeStruct((s, d), F32), jax.ShapeDtypeStruct((s // 2, d), U32),
                   jax.ShapeDtypeStruct((s, LANES), F32)],
        scratch_shapes=[pltpu.VMEM((tm, 2 * ATT_WIDTH), BF16)],
        compiler_params=_params(2),
        name="out_proj",
    )(a, b, ga, gb, w, x, g_ffn)


def _ffn_tile_start(j, tn, n_features):
    return pl.multiple_of(jnp.minimum(j * tn, n_features - tn), LANES)


def _ffn_up_kernel(xg_ref, rs_ref, w_hbm, cw_ref, cb_ref, o_ref, wg_stage, wv_stage, wg_scr, wv_scr, tail_scr,
                   sem, *, rows_per_chunk, n_features, last_shift):
    tm, tn = rs_ref.shape[0], o_ref.shape[1]

    def copies(jj):
        c0 = _ffn_tile_start(jj, tn, n_features)
        return [pltpu.make_async_copy(w_hbm.at[:, pl.ds(c0, tn)], wg_stage, sem.at[0]),
                pltpu.make_async_copy(w_hbm.at[:, pl.ds(n_features + c0, tn)], wv_stage, sem.at[1])]

    def cast():
        wg_scr[...] = wg_stage[...].astype(BF16)
        wv_scr[...] = wv_stage[...].astype(BF16)

    _stage_weight_tile(pl.program_id(0), pl.num_programs(0), copies, cast)

    @pl.when(pl.program_id(1) == 0)
    def _():
        tail_scr[...] = jnp.zeros(tail_scr.shape, F32)

    cw = cw_ref[...]
    cb = cb_ref[...]
    tail = tail_scr[...]
    sub = lax.broadcasted_iota(jnp.int32, tail.shape, 0)
    for r0 in range(0, tm, rows_per_chunk):
        rows = slice(r0, r0 + rows_per_chunk)
        packed_rows = slice(r0 // 2, (r0 + rows_per_chunk) // 2)
        xg = _unpack_rows(xg_ref[packed_rows, :])
        rs = jnp.concatenate([rs_ref[rows, :]] * (tn // LANES), axis=1)
        gate = _dot(xg, wg_scr[...]) * rs
        val = _dot(xg, wv_scr[...]) * rs

        def shifted(k, gate=gate, tail=tail):
            body = pltpu.roll(gate, k, axis=0)
            head = jnp.where(sub < k, pltpu.roll(tail, k, axis=0), body[:SUBLANES, :])
            return jnp.concatenate([head, body[SUBLANES:, :]], axis=0)

        conv = shifted(2) * cw[0:1, :] + shifted(1) * cw[1:2, :] + gate * cw[2:3, :] + cb
        act = conv * (1.0 / (1.0 + jnp.exp(-conv))) * val
        if last_shift:
            moved = jnp.concatenate([act[:, last_shift:], act[:, :last_shift]], axis=1)
            act = jnp.where(pl.program_id(0) == pl.num_programs(0) - 1, moved, act)
        o_ref[packed_rows, :] = _pack_rows(act.astype(BF16))
        tail = gate[rows_per_chunk - SUBLANES:, :]
    tail_scr[...] = tail


def _ffn_up(xg, rs, w_up, conv_w, conv_b, tm, tn):
    s, d = rs.shape[0], xg.shape[1]
    f = conv_w.shape[1]
    assert f % LANES == 0 and f >= tn
    n_f = -(-f // tn)
    start = lambda j: _ffn_tile_start(j, tn, f)
    return pl.pallas_call(
        functools.partial(_ffn_up_kernel, rows_per_chunk=min(128, tm), n_features=f, last_shift=n_f * tn - f),
        grid=(n_f, s // tm),
        in_specs=[
            pl.BlockSpec((tm // 2, d), lambda j, i: (i, 0)),
            pl.BlockSpec((tm, LANES), lambda j, i: (i, 0)),
            pl.BlockSpec(memory_space=pl.ANY),
            pl.BlockSpec((pl.Element(CONV_WIDTH), pl.Element(tn)), lambda j, i: (0, start(j))),
            pl.BlockSpec((pl.Element(1), pl.Element(tn)), lambda j, i: (0, start(j))),
        ],
        out_specs=pl.BlockSpec((tm // 2, tn), lambda j, i: (i, j)),
        out_shape=jax.ShapeDtypeStruct((s // 2, n_f * tn), U32),
        scratch_shapes=[pltpu.VMEM((d, tn), F32), pltpu.VMEM((d, tn), F32),
                        pltpu.VMEM((d, tn), BF16), pltpu.VMEM((d, tn), BF16),
                        pltpu.VMEM((SUBLANES, tn), F32), pltpu.SemaphoreType.DMA((2,))],
        compiler_params=_params(2),
        name="ffn_up",
    )(xg, rs, w_up, conv_w, conv_b)


def _ffn_down_kernel(a_ref, w_ref, x_ref, g_ref, o_ref, *, tn):
    n = pl.program_id(1)
    cols = pl.ds(pl.multiple_of(n * tn, tn), tn)
    o_ref[:, cols] = x_ref[...] + _dot(_unpack_rows(a_ref[...]), w_ref[...])

    @pl.when(n == pl.num_programs(1) - 1)
    def _():
        y = o_ref[...]
        o_ref[...] = y * _rms_scale(y) * g_ref[...]


def _ffn_down(act, w, x, g, tm, tn):
    s, d = x.shape
    f = w.shape[0]
    return pl.pallas_call(
        functools.partial(_ffn_down_kernel, tn=tn),
        grid=(s // tm, d // tn),
        in_specs=[
            pl.BlockSpec((tm // 2, f), lambda i, n: (i, 0)),
            pl.BlockSpec((f, tn), lambda i, n: (0, n)),
            pl.BlockSpec((tm, tn), lambda i, n: (i, n)),
            pl.BlockSpec((1, d), lambda i, n: (0, 0)),
        ],
        out_specs=pl.BlockSpec((tm, d), lambda i, n: (i, 0)),
        out_shape=jax.ShapeDtypeStruct((s, d), F32),
        compiler_params=_params(2, vmem_limit=60 * 1024 * 1024),
        name="ffn_down",
    )(act, w, x, g)


def _layer(x, attn_g, w_in, forget_b, bias_rows, fox_g, dil_g, w_out, ffn_g, w_up, conv_w, conv_b,
           w_down, final_g):
    s, d = x.shape
    c0 = 3 * ATT_WIDTH
    c1 = c0 + N_HEADS
    tm_small = min(512, s)
    tm_big = min(1024, s)

    w_in_t = w_in.T
    w_f = jnp.pad(w_in_t[c0:c1], ((0, LANES - N_HEADS), (0, 0))).astype(BF16)
    f_b = jnp.pad(forget_b.astype(F32), (0, LANES - N_HEADS))[None, :]
    h, kc = _norm_gate(x, attn_g[None, :], w_f, f_b, tm_small)

    qkv_a = _in_proj(h, w_in_t, 0, BF16, tm_big, 1024, "in_proj_fox")
    qkv_b = _in_proj(h, w_in_t, c1, F32, tm_big, 1024, "in_proj_dil")

    out_a, w_out_bf, w_down_bf = _fox_attention(qkv_a, kc, (w_out, w_down), min(2048, s), tm_big, 256,
                                                FOX_HEADS_PER_STEP)
    out_b = _dilated_attention(qkv_b, bias_rows)

    x1, xg, rs = _out_proj(out_a, out_b, fox_g[None, :], dil_g[None, :], w_out_bf, x,
                           ffn_g[None, :], tm_big, min(512, d))
    act = _ffn_up(xg, rs, w_up, conv_w, conv_b[None, :], tm_big, min(512, conv_w.shape[1]))
    return _ffn_down(act, w_down_bf, x1, final_g[None, :], tm_small, min(256, d))


def kernel(x, attn_norm_g, w_in, fox_forget_b, rel_bias_table, fox_out_norm_g, dil_out_norm_g, w_out,
           ffn_norm_g, w_up, conv_w, conv_b, w_down, final_norm_g):
    depth = w_in.shape[0]
    assert depth == 1 and x.shape[0] == 1, "single layer, single sequence"
    assert x.shape[1] % DIL_SUPER == 0 and conv_w.shape[-1] % 256 == 0
    bias_rows = _dil_bias_rows(rel_bias_table)
    y = _layer(x[0], attn_norm_g[0], w_in[0], fox_forget_b[0], bias_rows, fox_out_norm_g[0],
               dil_out_norm_g[0], w_out[0], ffn_norm_g[0], w_up[0], conv_w[0], conv_b[0], w_down[0],
               final_norm_g)
    return y[None]
```
